```python
import math
import jax
import jax.numpy as jnp
from jax import lax
import numpy as np

D_MODEL = 2048
BATCH = 4
SEQ = 2048
DEPTH = 1
DEC_BATCH = 8
DEC_SEQ = 8
PAST_LEN = 16384
PAGE_SIZE = 128

HEAD_DIM = 128
H_SB = 6
H_MOBA = 6
H_MEM = 4
W_SB = H_SB * HEAD_DIM
W_MOBA = H_MOBA * HEAD_DIM
W_MEM = H_MEM * HEAD_DIM
D_MIX = W_SB + W_MOBA + W_MEM
N_MEM = 256
MOBA_BLOCK = 256
MOBA_TOPK = 3
Q_BLOCK = 128
ROPE_THETA = 10000.0
LN_EPS = 1e-5
RMS_EPS = 1e-6
DEEPNORM_ALPHA = (2.0 * DEPTH) ** 0.25
DEEPNORM_BETA = (8.0 * DEPTH) ** -0.25
SPLIT_SIZES = (W_SB, W_SB, W_SB, W_SB, W_MOBA, W_MOBA, W_MOBA, W_MOBA, W_MEM, W_MEM)
SPLIT_POINTS = tuple(int(v) for v in np.cumsum(SPLIT_SIZES)[:-1])
D_IN = int(sum(SPLIT_SIZES))

kernel_name = "hymba_stickbreak_moba_memxattn_deepnorm_step"


def layer_norm(x, g, b):
    xf = x.astype(jnp.float32)
    mu = xf.mean(-1, keepdims=True)
    var = jnp.square(xf - mu).mean(-1, keepdims=True)
    return ((xf - mu) * lax.rsqrt(var + LN_EPS) * g + b).astype(x.dtype)


def group_rms(o, g):
    of = o.astype(jnp.float32)
    return (of * lax.rsqrt(jnp.mean(of * of, -1, keepdims=True) + RMS_EPS) * g).astype(o.dtype)


def rope(x, pos):
    inv = ROPE_THETA ** (-jnp.arange(0, HEAD_DIM, 2, dtype=jnp.float32) / HEAD_DIM)
    ang = pos.astype(jnp.float32)[:, None] * inv[None, :]
    cos = jnp.cos(ang)[None, :, None, :]
    sin = jnp.sin(ang)[None, :, None, :]
    xf = x.astype(jnp.float32)
    x1, x2 = xf[..., : HEAD_DIM // 2], xf[..., HEAD_DIM // 2:]
    return jnp.concatenate([x1 * cos - x2 * sin, x1 * sin + x2 * cos], -1).astype(x.dtype)


def project(x, pos, w_in):
    B, T, _ = x.shape
    h = jnp.einsum('btd,de->bte', x, w_in)
    q_a, k_a, v_a, g_a, q_b, k_b, v_b, g_b, q_m, g_m = jnp.split(h, SPLIT_POINTS, axis=-1)
    heads = lambda t, n: t.reshape(B, T, n, HEAD_DIM)
    q_b = rope(heads(q_b, H_MOBA), pos)
    k_b = rope(heads(k_b, H_MOBA), pos)
    return (heads(q_a, H_SB), heads(k_a, H_SB), heads(v_a, H_SB), g_a,
            q_b, k_b, heads(v_b, H_MOBA), g_b, heads(q_m, H_MEM), g_m)


def sb_attend(q, k, v, q_pos, k_pos):
    z = jnp.einsum('bqhd,bkhd->bhqk', q, k).astype(jnp.float32) * (HEAD_DIM ** -0.5)
    causal = (k_pos[None, :] < q_pos[:, None])[None, None]
    log_beta = jax.nn.log_sigmoid(z)
    log_1m = jnp.where(causal, jax.nn.log_sigmoid(-z), 0.0)
    rest = lax.cumsum(log_1m, axis=3, reverse=True) - log_1m
    a = jnp.where(causal, jnp.exp(log_beta + rest), 0.0)
    return jnp.einsum('bhqk,bkhd->bqhd', a.astype(v.dtype), v)


def moba_blocks(k, v):
    B, L, H, d = k.shape
    nb = -(-L // MOBA_BLOCK)
    pad = nb * MOBA_BLOCK - L
    def blk(t):
        t = jnp.pad(t, ((0, 0), (0, pad), (0, 0), (0, 0)))
        return t.reshape(B, nb, MOBA_BLOCK, H, d).transpose(0, 3, 1, 2, 4)
    kb, vb = blk(k), blk(v)
    kmean = kb.astype(jnp.float32).mean(axis=3)
    return kb, vb, kmean


def moba_attend(q, q_pos, kb, vb, kmean):
    B, Tq, H, d = q.shape
    nb = kb.shape[2]
    q_blk = q_pos // MOBA_BLOCK
    gate = jnp.einsum('bqhd,bhnd->bhqn', q.astype(jnp.float32), kmean)
    past = (jnp.arange(nb)[None, :] < q_blk[:, None])[None, None]
    gate = jnp.where(past, gate, -jnp.inf)
    k_top = min(MOBA_TOPK, nb)
    _, top_idx = lax.top_k(gate, k_top)
    own = jnp.broadcast_to(q_blk[None, None, :, None], (B, H, Tq, 1)).astype(top_idx.dtype)
    sel = jnp.concatenate([top_idx, own], -1)
    sel_ok = jnp.concatenate([top_idx < q_blk[None, None, :, None],
                              jnp.ones((B, H, Tq, 1), dtype=bool)], -1)
    n_sel = k_top + 1
    flat = sel.reshape(B, H, Tq * n_sel)
    b_idx = jnp.arange(B)[:, None, None]
    h_idx = jnp.arange(H)[None, :, None]
    k_sel = kb[b_idx, h_idx, flat].reshape(B, H, Tq, n_sel, MOBA_BLOCK, d)
    v_sel = vb[b_idx, h_idx, flat].reshape(B, H, Tq, n_sel, MOBA_BLOCK, d)
    key_pos = sel[..., None] * MOBA_BLOCK + jnp.arange(MOBA_BLOCK)
    mask = sel_ok[..., None] & (key_pos <= q_pos[None, None, :, None, None])
    logits = jnp.einsum('bqhd,bhqnkd->bhqnk', q, k_sel).astype(jnp.float32) * (HEAD_DIM ** -0.5)
    logits = jnp.where(mask, logits, -jnp.inf)
    p = jax.nn.softmax(logits.reshape(B, H, Tq, n_sel * MOBA_BLOCK), axis=-1)
    p = p.reshape(B, H, Tq, n_sel, MOBA_BLOCK)
    return jnp.einsum('bhqnk,bhqnkd->bqhd', p.astype(v_sel.dtype), v_sel)


def mem_attend(q, mk, mv):
    logits = jnp.einsum('bqhd,bmhd->bhqm', q, mk).astype(jnp.float32) * (HEAD_DIM ** -0.5)
    p = jax.nn.softmax(logits, axis=-1)
    return jnp.einsum('bhqm,bmhd->bqhd', p.astype(mv.dtype), mv)


def sweep_queries(fn, q, q_pos):
    B, T, H, d = q.shape
    n = T // Q_BLOCK
    qs = q.reshape(B, n, Q_BLOCK, H, d).swapaxes(0, 1)
    ps = q_pos.reshape(n, Q_BLOCK)
    out = lax.map(lambda qp: fn(qp[0], qp[1]), (qs, ps))
    return out.swapaxes(0, 1).reshape(B, T, H, d)


def merge(x, o_a, o_b, o_m, g_a, g_b, g_m, norm_a, norm_b, norm_m, w_out, ln_g, ln_b):
    B, T, _ = x.shape
    mix = jnp.concatenate([
        group_rms(o_a.reshape(B, T, W_SB), norm_a) * jax.nn.silu(g_a),
        group_rms(o_b.reshape(B, T, W_MOBA), norm_b) * jax.nn.silu(g_b),
        group_rms(o_m.reshape(B, T, W_MEM), norm_m) * jax.nn.silu(g_m)], -1)
    sub = jnp.einsum('bte,ed->btd', mix, w_out)
    return layer_norm(DEEPNORM_ALPHA * x + sub, ln_g, ln_b)


def gather_pages(pool, page_table):
    bd, npg = page_table.shape
    return pool[page_table].reshape(bd, npg * pool.shape[1], pool.shape[2], pool.shape[3])


def layer_prompt(x, mem, w_in, w_mem_k, w_mem_v, norm_a, norm_b, norm_m, w_out, ln_g, ln_b):
    B, T, _ = x.shape
    pos = jnp.arange(T, dtype=jnp.int32)
    q_a, k_a, v_a, g_a, q_b, k_b, v_b, g_b, q_m, g_m = project(x, pos, w_in)
    o_a = sweep_queries(lambda qb, pb: sb_attend(qb, k_a, v_a, pb, pos), q_a, pos)
    kb, vb, kmean = moba_blocks(k_b, v_b)
    o_b = sweep_queries(lambda qb, pb: moba_attend(qb, pb, kb, vb, kmean), q_b, pos)
    n_mem = mem.shape[1]
    mk = jnp.einsum('bmd,de->bme', mem, w_mem_k).reshape(B, n_mem, H_MEM, HEAD_DIM)
    mv = jnp.einsum('bmd,de->bme', mem, w_mem_v).reshape(B, n_mem, H_MEM, HEAD_DIM)
    o_m = mem_attend(q_m, mk, mv)
    y = merge(x, o_a, o_b, o_m, g_a, g_b, g_m, norm_a, norm_b, norm_m, w_out, ln_g, ln_b)
    return y, (k_a, v_a, k_b, v_b, mk, mv)


def layer_sample(x, sb_k_pool, sb_v_pool, moba_k_pool, moba_v_pool, mem_k, mem_v, page_table,
                 w_in, norm_a, norm_b, norm_m, w_out, ln_g, ln_b):
    B, T, _ = x.shape
    past_len = page_table.shape[1] * sb_k_pool.shape[1]
    pos = past_len + jnp.arange(T, dtype=jnp.int32)
    k_pos = jnp.arange(past_len + T, dtype=jnp.int32)
    q_a, k_a, v_a, g_a, q_b, k_b, v_b, g_b, q_m, g_m = project(x, pos, w_in)
    ka_all = jnp.concatenate([gather_pages(sb_k_pool, page_table), k_a], 1)
    va_all = jnp.concatenate([gather_pages(sb_v_pool, page_table), v_a], 1)
    o_a = sb_attend(q_a, ka_all, va_all, pos, k_pos)
    kb_all = jnp.concatenate([gather_pages(moba_k_pool, page_table), k_b], 1)
    vb_all = jnp.concatenate([gather_pages(moba_v_pool, page_table), v_b], 1)
    kb, vb, kmean = moba_blocks(kb_all, vb_all)
    o_b = moba_attend(q_b, pos, kb, vb, kmean)
    o_m = mem_attend(q_m, mem_k, mem_v)
    y = merge(x, o_a, o_b, o_m, g_a, g_b, g_m, norm_a, norm_b, norm_m, w_out, ln_g, ln_b)
    return y, (k_a, v_a, k_b, v_b)


def setup_inputs(seed: int = 0) -> dict:
    key = jax.random.key(seed)
    ks = jax.random.split(key, 32)
    f32 = jnp.float32
    n_pages = PAST_LEN // PAGE_SIZE
    n_used = DEC_BATCH * n_pages
    n_pool = n_used + max(1, n_used // 4)
    std = D_MODEL ** -0.5
    nrm = lambda k, shape: jax.random.normal(k, shape, f32)
    x_prompt = nrm(ks[0], (BATCH, SEQ, D_MODEL))
    x_sample = nrm(ks[1], (DEC_BATCH, DEC_SEQ, D_MODEL))
    cache_sb_k = nrm(ks[2], (DEPTH, n_pool, PAGE_SIZE, H_SB, HEAD_DIM))
    cache_sb_v = nrm(ks[3], (DEPTH, n_pool, PAGE_SIZE, H_SB, HEAD_DIM))
    cache_moba_k = nrm(ks[4], (DEPTH, n_pool, PAGE_SIZE, H_MOBA, HEAD_DIM))
    cache_moba_v = nrm(ks[5], (DEPTH, n_pool, PAGE_SIZE, H_MOBA, HEAD_DIM))
    cache_mem_k = nrm(ks[6], (DEPTH, DEC_BATCH, N_MEM, H_MEM, HEAD_DIM))
    cache_mem_v = nrm(ks[7], (DEPTH, DEC_BATCH, N_MEM, H_MEM, HEAD_DIM))
    page_table = jax.random.permutation(ks[8], n_pool)[:n_used].reshape(DEC_BATCH, n_pages).astype(jnp.int32)
    mem_prompt = nrm(ks[9], (BATCH, N_MEM, D_MODEL))
    col_scale = [1.0, 1.0, DEEPNORM_BETA, 1.0, 1.0, 1.0, DEEPNORM_BETA, 1.0, 1.0, 1.0]
    pieces = [nrm(ks[10 + i], (DEPTH, D_MODEL, w)) * (std * s)
              for i, (w, s) in enumerate(zip(SPLIT_SIZES, col_scale))]
    w_in = jnp.concatenate(pieces, axis=-1)
    w_mem_k = nrm(ks[20], (DEPTH, D_MODEL, W_MEM)) * std
    w_mem_v = nrm(ks[21], (DEPTH, D_MODEL, W_MEM)) * (std * DEEPNORM_BETA)
    norm_a = 1.0 + 0.02 * nrm(ks[22], (DEPTH, W_SB))
    norm_b = 1.0 + 0.02 * nrm(ks[23], (DEPTH, W_MOBA))
    norm_m = 1.0 + 0.02 * nrm(ks[24], (DEPTH, W_MEM))
    w_out = nrm(ks[25], (DEPTH, D_MIX, D_MODEL)) * (D_MIX ** -0.5 * DEEPNORM_BETA)
    ln_g = 1.0 + 0.02 * nrm(ks[26], (DEPTH, D_MODEL))
    ln_b = 0.02 * nrm(ks[27], (DEPTH, D_MODEL))
    return {"x_prompt": x_prompt, "x_sample": x_sample,
            "cache_sb_k": cache_sb_k, "cache_sb_v": cache_sb_v,
            "cache_moba_k": cache_moba_k, "cache_moba_v": cache_moba_v,
            "cache_mem_k": cache_mem_k, "cache_mem_v": cache_mem_v,
            "page_table": page_table, "mem_prompt": mem_prompt,
            "w_in": w_in, "w_mem_k": w_mem_k, "w_mem_v": w_mem_v,
            "norm_a": norm_a, "norm_b": norm_b, "norm_m": norm_m,
            "w_out": w_out, "ln_g": ln_g, "ln_b": ln_b}


def reference(x_prompt, x_sample, cache_sb_k, cache_sb_v, cache_moba_k, cache_moba_v,
              cache_mem_k, cache_mem_v, page_table, mem_prompt,
              w_in, w_mem_k, w_mem_v, norm_a, norm_b, norm_m, w_out, ln_g, ln_b):
    y_p, y_s = x_prompt, x_sample
    p_sb_k, p_sb_v, p_moba_k, p_moba_v, p_mem_k, p_mem_v = [], [], [], [], [], []
    s_sb_k, s_sb_v, s_moba_k, s_moba_v = [], [], [], []
    for l in range(DEPTH):
        y_p, (ka, va, kb, vb, mk, mv) = layer_prompt(
            y_p, mem_prompt, w_in[l], w_mem_k[l], w_mem_v[l],
            norm_a[l], norm_b[l], norm_m[l], w_out[l], ln_g[l], ln_b[l])
        p_sb_k.append(ka); p_sb_v.append(va); p_moba_k.append(kb); p_moba_v.append(vb)
        p_mem_k.append(mk); p_mem_v.append(mv)
        y_s, (ka_s, va_s, kb_s, vb_s) = layer_sample(
            y_s, cache_sb_k[l], cache_sb_v[l], cache_moba_k[l], cache_moba_v[l],
            cache_mem_k[l], cache_mem_v[l], page_table,
            w_in[l], norm_a[l], norm_b[l], norm_m[l], w_out[l], ln_g[l], ln_b[l])
        s_sb_k.append(ka_s); s_sb_v.append(va_s); s_moba_k.append(kb_s); s_moba_v.append(vb_s)
    new_p_sb_k = jnp.stack(p_sb_k); new_p_sb_v = jnp.stack(p_sb_v)
    new_p_moba_k = jnp.stack(p_moba_k); new_p_moba_v = jnp.stack(p_moba_v)
    new_p_mem_k = jnp.stack(p_mem_k); new_p_mem_v = jnp.stack(p_mem_v)
    new_s_sb_k = jnp.stack(s_sb_k); new_s_sb_v = jnp.stack(s_sb_v)
    new_s_moba_k = jnp.stack(s_moba_k); new_s_moba_v = jnp.stack(s_moba_v)
    return (y_p, y_s, new_p_sb_k, new_p_sb_v, new_p_moba_k, new_p_moba_v, new_p_mem_k, new_p_mem_v,
            new_s_sb_k, new_s_sb_v, new_s_moba_k, new_s_moba_v)
```

```python
import functools

import jax
import jax.numpy as jnp
import numpy as np
from jax import lax
from jax.experimental import pallas as pl
from jax.experimental.pallas import tpu as pltpu

F32 = jnp.float32
BF16 = jnp.bfloat16

HEAD_DIM = 128
MOBA_BLOCK = 256
MOBA_TOPK = 3
ROPE_THETA = 10000.0
LN_EPS = 1e-5
RMS_EPS = 1e-6
ATTN_SCALE = HEAD_DIM ** -0.5
NEG_BIG = -1e30
LANES = 128
VMEM_LIMIT = 56 * 1024 * 1024


def _cparams(sem):
    return pltpu.CompilerParams(dimension_semantics=sem, vmem_limit_bytes=VMEM_LIMIT)


def _dot_nt(a, b, precision=None):
    return lax.dot_general(a, b, (((1,), (1,)), ((), ())), precision=precision,
                           preferred_element_type=F32)


def _heads_to_lanes(ref, idx=()):
    n_heads = ref.shape[len(idx)]
    return jnp.concatenate([ref[idx + (h,)] for h in range(n_heads)], axis=1)


def _proj_kernel(*refs, segs, has_rope, tm, seq):
    x_ref, w_ref = refs[0], refs[1]
    pos = 2
    if has_rope:
        cos_ref, sin_ref = refs[2], refs[3]
        pos = 4
    outs = refs[pos:]
    x = x_ref[...].astype(BF16)
    rpb = min(tm, seq)
    o = 0
    for c0, width, rope, row_dtypes, head_dtypes, want_kmean in segs:
        r = jnp.dot(x, w_ref[:, c0:c0 + width], preferred_element_type=F32)
        n_heads = width // HEAD_DIM
        heads = [r[:, h * HEAD_DIM:(h + 1) * HEAD_DIM] for h in range(n_heads)]
        if rope:
            cos = cos_ref[...]
            sin = sin_ref[...]
            heads = [xh * cos + pltpu.roll(xh, HEAD_DIM // 2, axis=1) * sin for xh in heads]
            r = jnp.concatenate(heads, axis=1)
        for dt in row_dtypes:
            outs[o][...] = r.astype(dt)
            o += 1
        for dt in head_dtypes:
            for bb in range(tm // rpb):
                for h in range(n_heads):
                    outs[o][bb, h] = heads[h][bb * rpb:(bb + 1) * rpb].astype(dt)
            o += 1
        if want_kmean:
            nblk = tm // MOBA_BLOCK
            for h in range(n_heads):
                outs[o][0, 0, h] = jnp.sum(heads[h].reshape(nblk, MOBA_BLOCK, HEAD_DIM),
                                           axis=1) * (1.0 / MOBA_BLOCK)
            o += 1


def _project(x2d, w_bf, col_block, col_width, segs, tm, seq, rope_tabs=None):
    M, D = x2d.shape
    nt = M // tm
    n_batch = M // seq
    tpb = max(seq // tm, 1)
    rpb = min(tm, seq)
    has_rope = rope_tabs is not None
    in_specs = [pl.BlockSpec((tm, D), lambda i: (i, 0)),
                pl.BlockSpec((D, col_width), lambda i: (0, col_block))]
    args = [x2d, w_bf]
    if has_rope:
        in_specs += [pl.BlockSpec((tm, HEAD_DIM), lambda i: (i % tpb, 0))] * 2
        args += list(rope_tabs)
    out_shapes, out_specs = [], []
    for c0, width, rope, row_dtypes, head_dtypes, want_kmean in segs:
        n_heads = width // HEAD_DIM
        for dt in row_dtypes:
            out_shapes.append(jax.ShapeDtypeStruct((M, width), dt))
            out_specs.append(pl.BlockSpec((tm, width), lambda i: (i, 0)))
        for dt in head_dtypes:
            out_shapes.append(jax.ShapeDtypeStruct((n_batch, n_heads, seq, HEAD_DIM), dt))
            out_specs.append(pl.BlockSpec((tm // rpb, n_heads, rpb, HEAD_DIM),
                                          lambda i: (i // tpb, 0, i % tpb, 0)))
        if want_kmean:
            nblk = tm // MOBA_BLOCK
            out_shapes.append(jax.ShapeDtypeStruct((n_batch, tpb, n_heads, nblk, HEAD_DIM), F32))
            out_specs.append(pl.BlockSpec((1, 1, n_heads, nblk, HEAD_DIM),
                                          lambda i: (i // tpb, i % tpb, 0, 0, 0)))
    return pl.pallas_call(
        functools.partial(_proj_kernel, segs=segs, has_rope=has_rope, tm=tm, seq=seq),
        grid=(nt,), in_specs=in_specs, out_specs=out_specs, out_shape=out_shapes,
        compiler_params=_cparams(("parallel",)), name="proj",
    )(*args)


def _sb_tile(q, kt, vt, u2, carry_ref, acc_ref, mask):
    z = _dot_nt(q, kt) * ATTN_SCALE
    sp = jnp.log(1.0 + jnp.exp(-jnp.abs(z)))
    lb = jnp.minimum(z, 0.0) - sp
    l1m = lb - z
    if mask is not None:
        l1m = jnp.where(mask, l1m, 0.0)
    hi = l1m.astype(BF16)
    lo = (l1m - hi.astype(F32)).astype(BF16)
    rest = jnp.dot(jnp.concatenate([hi, lo], axis=1), u2, preferred_element_type=F32)
    carry = carry_ref[...]
    reps = z.shape[1] // LANES
    a = jnp.exp(lb + rest + jnp.tile(carry, (1, reps)))
    if mask is not None:
        a = jnp.where(mask, a, 0.0)
    acc_ref[...] += jnp.dot(a.astype(BF16), vt, preferred_element_type=F32)
    tot = rest[:, :1] + l1m[:, :1]
    carry_ref[...] = carry + tot


def _sb_prompt_kernel(q_ref, k_ref, v_ref, u_ref, o_ref, carry_ref, acc_ref, *, tq):
    i = pl.program_id(2)
    q = q_ref[0, 0]
    u2 = u_ref[...]
    carry_ref[...] = jnp.zeros_like(carry_ref)
    acc_ref[...] = jnp.zeros_like(acc_ref)
    row = lax.broadcasted_iota(jnp.int32, (tq, tq), 0)
    col = lax.broadcasted_iota(jnp.int32, (tq, tq), 1)
    d0 = pl.multiple_of(i * tq, tq)
    _sb_tile(q, k_ref[0, 0, pl.ds(d0, tq), :], v_ref[0, 0, pl.ds(d0, tq), :], u2,
             carry_ref, acc_ref, col < row)

    def body(s, c):
        j0 = pl.multiple_of((i - 1 - s) * tq, tq)
        _sb_tile(q, k_ref[0, 0, pl.ds(j0, tq), :], v_ref[0, 0, pl.ds(j0, tq), :], u2,
                 carry_ref, acc_ref, None)
        return c

    lax.fori_loop(0, i, body, 0)
    o_ref[0, 0] = acc_ref[...]


def _suffix_matrix(n):
    u = (np.arange(n)[:, None] > np.arange(n)[None, :]).astype(np.float32)
    return jnp.asarray(np.concatenate([u, u], axis=0), dtype=BF16)


def _sb_prompt(q_bf, k_bf, v_bf, tq=256):
    B, H, T, _ = q_bf.shape
    q_spec = pl.BlockSpec((1, 1, tq, HEAD_DIM), lambda b, h, i: (b, h, i, 0))
    kv_spec = pl.BlockSpec((1, 1, T, HEAD_DIM), lambda b, h, i: (b, h, 0, 0))
    return pl.pallas_call(
        functools.partial(_sb_prompt_kernel, tq=tq),
        grid=(B, H, T // tq),
        in_specs=[q_spec, kv_spec, kv_spec, pl.BlockSpec((2 * tq, tq), lambda b, h, i: (0, 0))],
        out_specs=q_spec,
        out_shape=jax.ShapeDtypeStruct((B, H, T, HEAD_DIM), F32),
        scratch_shapes=[pltpu.VMEM((tq, LANES), F32), pltpu.VMEM((tq, HEAD_DIM), F32)],
        compiler_params=_cparams(("parallel", "parallel", "arbitrary")), name="sb_prompt",
    )(q_bf, k_bf, v_bf, _suffix_matrix(tq))


def _select_topk(gate, past):
    lane = lax.broadcasted_iota(jnp.int32, gate.shape, 1)
    g = jnp.where(past, gate, -jnp.inf)
    sel = jnp.zeros(gate.shape, F32)
    for _ in range(MOBA_TOPK):
        m = jnp.max(g, axis=1, keepdims=True)
        idx = jnp.min(jnp.where(g == m, lane, LANES), axis=1, keepdims=True)
        hit = lane == idx
        sel = jnp.where(hit & (m > -jnp.inf), 1.0, sel)
        g = jnp.where(hit, -jnp.inf, g)
    return sel


def _moba_prompt_kernel(q_ref, k_ref, v_ref, km_ref, o_ref, m_ref, l_ref, acc_ref, *, tq, nb):
    i = pl.program_id(2)
    qf = q_ref[0, 0]
    q = qf.astype(BF16)
    km = jnp.concatenate([km_ref[0, 0], jnp.zeros((LANES - nb, HEAD_DIM), F32)], axis=0)
    gate = _dot_nt(qf, km, precision=lax.Precision.HIGHEST)
    lane = lax.broadcasted_iota(jnp.int32, (tq, LANES), 1)
    sel = _select_topk(gate, lane < i)

    row = lax.broadcasted_iota(jnp.int32, (tq, tq), 0)
    col = lax.broadcasted_iota(jnp.int32, (tq, tq), 1)
    d0 = pl.multiple_of(i * tq, tq)
    s = jnp.where(col <= row, _dot_nt(q, k_ref[0, 0, pl.ds(d0, tq), :]) * ATTN_SCALE, NEG_BIG)
    m = jnp.max(s, axis=1, keepdims=True)
    p = jnp.exp(s - m)
    m_ref[...] = m
    l_ref[...] = jnp.sum(p, axis=1, keepdims=True)
    acc_ref[...] = jnp.dot(p.astype(BF16), v_ref[0, 0, pl.ds(d0, tq), :],
                           preferred_element_type=F32)

    for n in range(nb - 1):
        @pl.when(n < i)
        def _():
            keep = sel[:, n:n + 1] > 0.0
            s = jnp.where(keep, _dot_nt(q, k_ref[0, 0, n * tq:(n + 1) * tq, :]) * ATTN_SCALE,
                          NEG_BIG)
            m_old = m_ref[...]
            m_new = jnp.maximum(m_old, jnp.max(s, axis=1, keepdims=True))
            alpha = jnp.exp(m_old - m_new)
            p = jnp.where(keep, jnp.exp(s - m_new), 0.0)
            m_ref[...] = m_new
            l_ref[...] = alpha * l_ref[...] + jnp.sum(p, axis=1, keepdims=True)
            acc_ref[...] = alpha * acc_ref[...] + jnp.dot(
                p.astype(BF16), v_ref[0, 0, n * tq:(n + 1) * tq, :], preferred_element_type=F32)

    o_ref[0, 0] = acc_ref[...] / l_ref[...]


def _moba_prompt(q_f32, k_bf, v_bf, kmean):
    B, H, T, _ = q_f32.shape
    tq = MOBA_BLOCK
    nb = T // tq
    q_spec = pl.BlockSpec((1, 1, tq, HEAD_DIM), lambda b, h, i: (b, h, i, 0))
    kv_spec = pl.BlockSpec((1, 1, T, HEAD_DIM), lambda b, h, i: (b, h, 0, 0))
    return pl.pallas_call(
        functools.partial(_moba_prompt_kernel, tq=tq, nb=nb),
        grid=(B, H, nb),
        in_specs=[q_spec, kv_spec, kv_spec,
                  pl.BlockSpec((1, 1, nb, HEAD_DIM), lambda b, h, i: (b, h, 0, 0))],
        out_specs=q_spec,
        out_shape=jax.ShapeDtypeStruct((B, H, T, HEAD_DIM), F32),
        scratch_shapes=[pltpu.VMEM((tq, 1), F32), pltpu.VMEM((tq, 1), F32),
                        pltpu.VMEM((tq, HEAD_DIM), F32)],
        compiler_params=_cparams(("parallel", "parallel", "arbitrary")), name="moba_prompt",
    )(q_f32, k_bf, v_bf, kmean)


def _mem_attn_kernel(q_ref, k_ref, v_ref, o_ref):
    s = _dot_nt(q_ref[0, 0].astype(BF16), k_ref[0].astype(BF16)) * ATTN_SCALE
    m = jnp.max(s, axis=1, keepdims=True)
    p = jnp.exp(s - m)
    l = jnp.sum(p, axis=1, keepdims=True)
    o = jnp.dot(p.astype(BF16), v_ref[0].astype(BF16), preferred_element_type=F32)
    o_ref[0, 0] = o / l


def _mem_attn(q, mk, mv, tq):
    B, H, T, _ = q.shape
    n_mem = mk.shape[1]
    q_spec = pl.BlockSpec((1, 1, tq, HEAD_DIM), lambda b, h, i: (b, h, i, 0))
    kv_spec = pl.BlockSpec((1, n_mem, HEAD_DIM), lambda b, h, i: (b, 0, h))
    return pl.pallas_call(
        _mem_attn_kernel,
        grid=(B, H, T // tq),
        in_specs=[q_spec, kv_spec, kv_spec],
        out_specs=q_spec,
        out_shape=jax.ShapeDtypeStruct((B, H, T, HEAD_DIM), F32),
        compiler_params=_cparams(("parallel", "parallel", "arbitrary")), name="mem_attn",
    )(q, mk, mv)


def _merge_kernel(x_ref, oa_ref, ob_ref, om_ref, ga_ref, gb_ref, gm_ref,
                  na_ref, nb_ref, nm_ref, w_ref, lg_ref, lb_ref, y_ref, *, alpha):
    def group(o_ref, g_ref, n_ref):
        o = jnp.concatenate([_heads_to_lanes(o_ref, (bb,)) for bb in range(o_ref.shape[0])],
                            axis=0)
        r = o * lax.rsqrt(jnp.mean(o * o, axis=-1, keepdims=True) + RMS_EPS) * n_ref[...]
        g = g_ref[...]
        return (r * (g * (1.0 / (1.0 + jnp.exp(-g))))).astype(BF16)

    mix = jnp.concatenate([group(oa_ref, ga_ref, na_ref), group(ob_ref, gb_ref, nb_ref),
                           group(om_ref, gm_ref, nm_ref)], axis=1)
    sub = jnp.dot(mix, w_ref[...], preferred_element_type=F32)
    h = alpha * x_ref[...] + sub
    mu = jnp.mean(h, axis=-1, keepdims=True)
    d = h - mu
    var = jnp.mean(d * d, axis=-1, keepdims=True)
    y_ref[...] = d * lax.rsqrt(var + LN_EPS) * lg_ref[...] + lb_ref[...]


def _merge(x2d, o_a, o_b, o_m, g_a, g_b, g_m, norm_a, norm_b, norm_m, w_out_bf, ln_g, ln_b,
           alpha, tm):
    M, D = x2d.shape
    seq = o_a.shape[2]
    tpb = max(seq // tm, 1)
    rpb = min(tm, seq)
    row = lambda a: pl.BlockSpec((tm, a.shape[1]), lambda i: (i, 0))
    head = lambda a: pl.BlockSpec((tm // rpb, a.shape[1], rpb, HEAD_DIM),
                                  lambda i: (i // tpb, 0, i % tpb, 0))
    full = lambda a: pl.BlockSpec(a.shape, lambda i: (0, 0))
    vecs = [v.reshape(1, -1) for v in (norm_a, norm_b, norm_m, ln_g, ln_b)]
    args = [x2d, o_a, o_b, o_m, g_a, g_b, g_m, *vecs[:3], w_out_bf, *vecs[3:]]
    in_specs = ([row(x2d)] + [head(a) for a in (o_a, o_b, o_m)] + [row(a) for a in (g_a, g_b, g_m)]
                + [full(a) for a in args[7:]])
    return pl.pallas_call(
        functools.partial(_merge_kernel, alpha=alpha),
        grid=(M // tm,), in_specs=in_specs,
        out_specs=pl.BlockSpec((tm, D), lambda i: (i, 0)),
        out_shape=jax.ShapeDtypeStruct((M, D), F32),
        compiler_params=_cparams(("parallel",)), name="merge",
    )(*args)


def _block_diag_queries(q):
    B, H, Tq, d = q.shape
    eye = jnp.eye(H, dtype=q.dtype)
    return (q[:, :, :, None, :] * eye[None, :, None, :, None]).reshape(B, H * Tq, H * d)


def _new_token_rows(t):
    B, H, Tq, d = t.shape
    rows = jnp.transpose(t, (0, 2, 1, 3)).reshape(B, Tq, H * d)
    return jnp.pad(rows, ((0, 0), (0, LANES - Tq), (0, 0)))


def _pages_to_rows(p0_ref, p1_ref, dtype):
    return jnp.concatenate([_heads_to_lanes(p0_ref).astype(dtype),
                            _heads_to_lanes(p1_ref).astype(dtype)], axis=0)


def _take_block_diag(acc, o_ref, n_heads, tq):
    for h in range(n_heads):
        o_ref[0, h] = acc[h * tq:(h + 1) * tq, h * HEAD_DIM:(h + 1) * HEAD_DIM]


def _sb_sample_kernel(pt_ref, q_ref, k0_ref, k1_ref, v0_ref, v1_ref, kn_ref, vn_ref,
                      u_ref, un_ref, o_ref, carry_ref, acc_ref, *, n_heads, tq, n_steps):
    p = pl.program_id(1)
    q = q_ref[0]
    rows = n_heads * tq

    @pl.when(p == 0)
    def _():
        carry_ref[...] = jnp.zeros_like(carry_ref)
        acc_ref[...] = jnp.zeros_like(acc_ref)
        row = lax.broadcasted_iota(jnp.int32, (rows, LANES), 0)
        col = lax.broadcasted_iota(jnp.int32, (rows, LANES), 1)
        _sb_tile(q, kn_ref[0].astype(BF16), vn_ref[0].astype(BF16), un_ref[...],
                 carry_ref, acc_ref, col < row % tq)

    _sb_tile(q, _pages_to_rows(k0_ref, k1_ref, BF16), _pages_to_rows(v0_ref, v1_ref, BF16),
             u_ref[...], carry_ref, acc_ref, None)

    @pl.when(p == n_steps - 1)
    def _():
        _take_block_diag(acc_ref[...], o_ref, n_heads, tq)


def _sb_sample(q_bd_bf, k_pool, v_pool, k_new_rows, v_new_rows, page_table, tq):
    B, rows, W = q_bd_bf.shape
    _, H, page, _ = k_pool.shape
    n_steps = page_table.shape[1] // 2
    pg = lambda off: pl.BlockSpec(
        (None, H, page, HEAD_DIM),
        lambda b, p, pt: (pt[b, 2 * (n_steps - 1 - p) + off], 0, 0, 0))
    per_b = lambda a: pl.BlockSpec((1,) + a.shape[1:], lambda b, p, pt: (b, 0, 0))
    u2, un2 = _suffix_matrix(2 * page), _suffix_matrix(LANES)
    const = lambda a: pl.BlockSpec(a.shape, lambda b, p, pt: (0, 0))
    grid_spec = pltpu.PrefetchScalarGridSpec(
        num_scalar_prefetch=1, grid=(B, n_steps),
        in_specs=[per_b(q_bd_bf), pg(0), pg(1), pg(0), pg(1), per_b(k_new_rows),
                  per_b(v_new_rows), const(u2), const(un2)],
        out_specs=pl.BlockSpec((1, H, tq, HEAD_DIM), lambda b, p, pt: (b, 0, 0, 0)),
        scratch_shapes=[pltpu.VMEM((rows, LANES), F32), pltpu.VMEM((rows, W), F32)])
    return pl.pallas_call(
        functools.partial(_sb_sample_kernel, n_heads=H, tq=tq, n_steps=n_steps),
        grid_spec=grid_spec, out_shape=jax.ShapeDtypeStruct((B, H, tq, HEAD_DIM), F32),
        compiler_params=_cparams(("parallel", "arbitrary")), name="sb_sample",
    )(page_table, q_bd_bf, k_pool, k_pool, v_pool, v_pool, k_new_rows, v_new_rows, u2, un2)


def _moba_sample_kernel(pt_ref, qf_ref, k0_ref, k1_ref, v0_ref, v1_ref, kn_ref, vn_ref, o_ref,
                        s_ref, p_ref, km_ref, pn_ref, l_ref, acc_ref, *, n_heads, tq, nb):
    ph = pl.program_id(1)
    n = pl.program_id(2)
    rows = n_heads * tq
    blk = MOBA_BLOCK

    @pl.when((ph == 0) & (n == 0))
    def _():
        km_ref[...] = jnp.zeros_like(km_ref)

    @pl.when(ph == 0)
    def _():
        kc = _pages_to_rows(k0_ref, k1_ref, F32)
        km_ref[pl.ds(n, 1), :] = jnp.sum(kc, axis=0, keepdims=True) * (1.0 / blk)
        s_ref[n] = _dot_nt(qf_ref[0].astype(BF16), kc.astype(BF16)) * ATTN_SCALE

    @pl.when((ph == 0) & (n == nb - 1))
    def _():
        qf = qf_ref[0]
        gate = _dot_nt(qf, km_ref[...], precision=lax.Precision.HIGHEST)
        lane = lax.broadcasted_iota(jnp.int32, (rows, LANES), 1)
        sel = _select_topk(gate, lane < nb)
        row = lax.broadcasted_iota(jnp.int32, (rows, LANES), 0)
        s_new = jnp.where(lane <= row % tq,
                          _dot_nt(qf.astype(BF16), kn_ref[0].astype(BF16)) * ATTN_SCALE, NEG_BIG)
        m = jnp.max(s_new, axis=1, keepdims=True)
        for j in range(nb):
            sj = jnp.where(sel[:, j:j + 1] > 0.0, s_ref[j], NEG_BIG)
            m = jnp.maximum(m, jnp.max(sj, axis=1, keepdims=True))
        p_new = jnp.exp(s_new - m)
        l = jnp.sum(p_new, axis=1, keepdims=True)
        pn_ref[...] = p_new.astype(BF16)
        for j in range(nb):
            keep = sel[:, j:j + 1] > 0.0
            pj = jnp.where(keep, jnp.exp(jnp.where(keep, s_ref[j], NEG_BIG) - m), 0.0)
            l = l + jnp.sum(pj, axis=1, keepdims=True)
            p_ref[j] = pj.astype(BF16)
        l_ref[...] = l
        acc_ref[...] = jnp.dot(pn_ref[...], vn_ref[0].astype(BF16), preferred_element_type=F32)

    @pl.when(ph == 1)
    def _():
        acc_ref[...] += jnp.dot(p_ref[n], _pages_to_rows(v0_ref, v1_ref, BF16),
                                preferred_element_type=F32)

    @pl.when((ph == 1) & (n == nb - 1))
    def _():
        _take_block_diag(acc_ref[...] / l_ref[...], o_ref, n_heads, tq)


def _moba_sample(q_bd_f32, k_pool, v_pool, k_new_rows, v_new_rows, page_table, tq):
    B, rows, W = q_bd_f32.shape
    _, H, page, _ = k_pool.shape
    assert 2 * page == MOBA_BLOCK
    nb = page_table.shape[1] // 2
    assert nb <= LANES
    kpg = lambda off: pl.BlockSpec(
        (None, H, page, HEAD_DIM),
        lambda b, ph, n, pt: (pt[b, 2 * jnp.where(ph == 0, n, nb - 1) + off], 0, 0, 0))
    vpg = lambda off: pl.BlockSpec(
        (None, H, page, HEAD_DIM),
        lambda b, ph, n, pt: (pt[b, 2 * jnp.where(ph == 0, 0, n) + off], 0, 0, 0))
    per_b = lambda a: pl.BlockSpec((1,) + a.shape[1:], lambda b, ph, n, pt: (b, 0, 0))
    grid_spec = pltpu.PrefetchScalarGridSpec(
        num_scalar_prefetch=1, grid=(B, 2, nb),
        in_specs=[per_b(q_bd_f32), kpg(0), kpg(1), vpg(0), vpg(1), per_b(k_new_rows),
                  per_b(v_new_rows)],
        out_specs=pl.BlockSpec((1, H, tq, HEAD_DIM), lambda b, ph, n, pt: (b, 0, 0, 0)),
        scratch_shapes=[pltpu.VMEM((nb, rows, MOBA_BLOCK), F32),
                        pltpu.VMEM((nb, rows, MOBA_BLOCK), BF16),
                        pltpu.VMEM((LANES, W), F32),
                        pltpu.VMEM((rows, LANES), BF16),
                        pltpu.VMEM((rows, 1), F32),
                        pltpu.VMEM((rows, W), F32)])
    return pl.pallas_call(
        functools.partial(_moba_sample_kernel, n_heads=H, tq=tq, nb=nb),
        grid_spec=grid_spec, out_shape=jax.ShapeDtypeStruct((B, H, tq, HEAD_DIM), F32),
        compiler_params=_cparams(("parallel", "arbitrary", "arbitrary")), name="moba_sample",
    )(page_table, q_bd_f32, k_pool, k_pool, v_pool, v_pool, k_new_rows, v_new_rows)


def _rope_tables(pos):
    inv = ROPE_THETA ** (-jnp.arange(0, HEAD_DIM, 2, dtype=F32) / HEAD_DIM)
    ang = pos.astype(F32)[:, None] * inv[None, :]
    cos, sin = jnp.cos(ang), jnp.sin(ang)
    return jnp.concatenate([cos, cos], -1), jnp.concatenate([-sin, sin], -1)


def _project_all(x2d, w_in_bf, w_sb, w_moba, w_mem, tm, seq, rope_tabs, prompt):
    assert w_sb == w_moba and (8 * w_sb) % (2 * w_mem) == 0
    grp = 4 * w_sb
    kv = (F32, BF16) if prompt else (F32,)
    seg = lambda k, width, rope=False, rows=(), heads=(), km=False: (
        k * width, width, rope, rows, heads, km)
    a = _project(x2d, w_in_bf, 0, grp,
                 (seg(0, w_sb, heads=(BF16,) if prompt else (F32,)), seg(1, w_sb, heads=kv),
                  seg(2, w_sb, heads=kv), seg(3, w_sb, rows=(F32,))), tm, seq)
    b = _project(x2d, w_in_bf, 1, grp,
                 (seg(0, w_moba, rope=True, heads=(F32,)),
                  seg(1, w_moba, rope=True, heads=kv, km=prompt),
                  seg(2, w_moba, heads=kv), seg(3, w_moba, rows=(F32,))), tm, seq,
                 rope_tabs=rope_tabs)
    qm, gm = _project(x2d, w_in_bf, (2 * grp) // (2 * w_mem), 2 * w_mem,
                      (seg(0, w_mem, heads=(F32,)), seg(1, w_mem, rows=(F32,))), tm, seq)
    if prompt:
        names_a = ("qa", "ka", "ka_bf", "va", "va_bf", "ga")
        names_b = ("qb", "kb", "kb_bf", "kmean", "vb", "vb_bf", "gb")
    else:
        names_a = ("qa", "ka", "va", "ga")
        names_b = ("qb", "kb", "vb", "gb")
    out = dict(zip(names_a, a))
    out.update(zip(names_b, b))
    out.update(qm=qm, gm=gm)
    return out


def kernel(x_prompt, x_sample, cache_sb_k, cache_sb_v, cache_moba_k, cache_moba_v,
           cache_mem_k, cache_mem_v, page_table, mem_prompt,
           w_in, w_mem_k, w_mem_v, norm_a, norm_b, norm_m, w_out, ln_g, ln_b):
    depth = w_in.shape[0]
    B, T, D = x_prompt.shape
    Bs, Ts, _ = x_sample.shape
    h_sb, h_moba, h_mem = cache_sb_k.shape[3], cache_moba_k.shape[3], cache_mem_k.shape[3]
    w_sb, w_moba, w_mem = h_sb * HEAD_DIM, h_moba * HEAD_DIM, h_mem * HEAD_DIM
    n_mem = mem_prompt.shape[1]
    page = cache_sb_k.shape[2]
    past_len = page_table.shape[1] * page
    assert past_len % MOBA_BLOCK == 0 and Ts <= LANES and T % MOBA_BLOCK == 0
    alpha = (2.0 * depth) ** 0.25

    tm = 512
    rope_p = _rope_tables(jnp.arange(T, dtype=jnp.int32))
    rope_s = tuple(jnp.tile(t, (Bs, 1)) for t in
                   _rope_tables(past_len + jnp.arange(Ts, dtype=jnp.int32)))

    y_p = x_prompt.reshape(B * T, D)
    y_s = x_sample.reshape(Bs * Ts, D)
    outs = [[] for _ in range(10)]
    for l in range(depth):
        w_in_bf = w_in[l].astype(BF16)
        w_out_bf = w_out[l].astype(BF16)
        w_memkv_bf = jnp.concatenate([w_mem_k[l], w_mem_v[l]], axis=1).astype(BF16)

        pp = _project_all(y_p, w_in_bf, w_sb, w_moba, w_mem, tm, T, rope_p, True)
        o_a = _sb_prompt(pp["qa"], pp["ka_bf"], pp["va_bf"])
        kmean = jnp.transpose(pp["kmean"], (0, 2, 1, 3, 4)).reshape(
            B, h_moba, T // MOBA_BLOCK, HEAD_DIM)
        o_b = _moba_prompt(pp["qb"], pp["kb_bf"], pp["vb_bf"], kmean)
        mk, mv = _project(
            mem_prompt.reshape(B * n_mem, D), w_memkv_bf, 0, 2 * w_mem,
            ((0, w_mem, False, (F32,), (), False), (w_mem, w_mem, False, (F32,), (), False)),
            256, n_mem)
        o_m = _mem_attn(pp["qm"], mk.reshape(B, n_mem, w_mem), mv.reshape(B, n_mem, w_mem), 512)
        y_p_new = _merge(y_p, o_a, o_b, o_m, pp["ga"], pp["gb"], pp["gm"],
                         norm_a[l], norm_b[l], norm_m[l], w_out_bf, ln_g[l], ln_b[l], alpha, 256)

        ps = _project_all(y_s, w_in_bf, w_sb, w_moba, w_mem, Bs * Ts, Ts, rope_s, False)
        pool = lambda c: jnp.transpose(c[l], (0, 2, 1, 3))
        so_a = _sb_sample(_block_diag_queries(ps["qa"]).astype(BF16), pool(cache_sb_k),
                          pool(cache_sb_v), _new_token_rows(ps["ka"]), _new_token_rows(ps["va"]),
                          page_table, Ts)
        so_b = _moba_sample(_block_diag_queries(ps["qb"]), pool(cache_moba_k),
                            pool(cache_moba_v), _new_token_rows(ps["kb"]),
                            _new_token_rows(ps["vb"]), page_table, Ts)
        so_m = _mem_attn(ps["qm"], cache_mem_k[l].reshape(Bs, n_mem, w_mem),
                         cache_mem_v[l].reshape(Bs, n_mem, w_mem), Ts)
        y_s_new = _merge(y_s, so_a, so_b, so_m, ps["ga"], ps["gb"], ps["gm"],
                         norm_a[l], norm_b[l], norm_m[l], w_out_bf, ln_g[l], ln_b[l], alpha,
                         Bs * Ts)

        tok_major = lambda a: jnp.transpose(a, (0, 2, 1, 3))
        new = [tok_major(pp["ka"]), tok_major(pp["va"]), tok_major(pp["kb"]), tok_major(pp["vb"]),
               mk.reshape(B, n_mem, h_mem, HEAD_DIM), mv.reshape(B, n_mem, h_mem, HEAD_DIM),
               tok_major(ps["ka"]), tok_major(ps["va"]), tok_major(ps["kb"]), tok_major(ps["vb"])]
        for lst, a in zip(outs, new):
            lst.append(a)
        y_p, y_s = y_p_new, y_s_new

    return (y_p.reshape(B, T, D), y_s.reshape(Bs, Ts, D), *[jnp.stack(o) for o in outs])
```

```python
import functools

import jax
import jax.numpy as jnp
import numpy as np
from jax import lax
from jax.experimental import pallas as pl
from jax.experimental.pallas import tpu as pltpu

F32 = jnp.float32
BF16 = jnp.bfloat16

HEAD_DIM = 128
MOBA_BLOCK = 256
MOBA_TOPK = 3
ROPE_THETA = 10000.0
LN_EPS = 1e-5
RMS_EPS = 1e-6
ATTN_SCALE = HEAD_DIM ** -0.5
NEG_BIG = -1e30
LANES = 128
VMEM_LIMIT = 56 * 1024 * 1024
SAMPLE_PAGES_PER_STEP = 8


def _cparams(sem):
    return pltpu.CompilerParams(dimension_semantics=sem, vmem_limit_bytes=VMEM_LIMIT)


def _dot_nt(a, b, precision=None):
    return lax.dot_general(a, b, (((1,), (1,)), ((), ())), precision=precision,
                           preferred_element_type=F32)


def _heads_to_lanes(ref, idx=()):
    n_heads = ref.shape[len(idx)]
    return jnp.concatenate([ref[idx + (h,)] for h in range(n_heads)], axis=1)


def _proj_kernel(*refs, segs, has_rope, tm, seq):
    x_ref, w_ref = refs[0], refs[1]
    pos = 2
    if has_rope:
        cos_ref, sin_ref = refs[2], refs[3]
        pos = 4
    outs = refs[pos:]
    x = x_ref[...].astype(BF16)
    rpb = min(tm, seq)
    o = 0
    for c0, width, rope, row_dtypes, head_dtypes, headt_dtypes, want_kmean in segs:
        r = jnp.dot(x, w_ref[:, c0:c0 + width], preferred_element_type=F32)
        n_heads = width // HEAD_DIM
        heads = [r[:, h * HEAD_DIM:(h + 1) * HEAD_DIM] for h in range(n_heads)]
        if rope:
            cos = cos_ref[...]
            sin = sin_ref[...]
            heads = [xh * cos + pltpu.roll(xh, HEAD_DIM // 2, axis=1) * sin for xh in heads]
            r = jnp.concatenate(heads, axis=1)
        for dt in row_dtypes:
            outs[o][...] = r.astype(dt)
            o += 1
        for dt in head_dtypes:
            for bb in range(tm // rpb):
                for h in range(n_heads):
                    outs[o][bb, h] = heads[h][bb * rpb:(bb + 1) * rpb].astype(dt)
            o += 1
        for dt in headt_dtypes:
            for h in range(n_heads):
                outs[o][0, h] = heads[h].T.astype(dt)
            o += 1
        if want_kmean:
            nblk = tm // MOBA_BLOCK
            for h in range(n_heads):
                outs[o][0, 0, h] = jnp.sum(heads[h].reshape(nblk, MOBA_BLOCK, HEAD_DIM),
                                           axis=1) * (1.0 / MOBA_BLOCK)
            o += 1


def _project(x2d, w_bf, col_block, col_width, segs, tm, seq, rope_tabs=None):
    M, D = x2d.shape
    nt = M // tm
    n_batch = M // seq
    tpb = max(seq // tm, 1)
    rpb = min(tm, seq)
    has_rope = rope_tabs is not None
    in_specs = [pl.BlockSpec((tm, D), lambda i: (i, 0)),
                pl.BlockSpec((D, col_width), lambda i: (0, col_block))]
    args = [x2d, w_bf]
    if has_rope:
        in_specs += [pl.BlockSpec((tm, HEAD_DIM), lambda i: (i % tpb, 0))] * 2
        args += list(rope_tabs)
    out_shapes, out_specs = [], []
    for c0, width, rope, row_dtypes, head_dtypes, headt_dtypes, want_kmean in segs:
        n_heads = width // HEAD_DIM
        for dt in row_dtypes:
            out_shapes.append(jax.ShapeDtypeStruct((M, width), dt))
            out_specs.append(pl.BlockSpec((tm, width), lambda i: (i, 0)))
        for dt in head_dtypes:
            out_shapes.append(jax.ShapeDtypeStruct((n_batch, n_heads, seq, HEAD_DIM), dt))
            out_specs.append(pl.BlockSpec((tm // rpb, n_heads, rpb, HEAD_DIM),
                                          lambda i: (i // tpb, 0, i % tpb, 0)))
        for dt in headt_dtypes:
            assert tm <= seq
            out_shapes.append(jax.ShapeDtypeStruct((n_batch, n_heads, HEAD_DIM, seq), dt))
            out_specs.append(pl.BlockSpec((1, n_heads, HEAD_DIM, tm),
                                          lambda i: (i // tpb, 0, 0, i % tpb)))
        if want_kmean:
            nblk = tm // MOBA_BLOCK
            out_shapes.append(jax.ShapeDtypeStruct((n_batch, tpb, n_heads, nblk, HEAD_DIM), F32))
            out_specs.append(pl.BlockSpec((1, 1, n_heads, nblk, HEAD_DIM),
                                          lambda i: (i // tpb, i % tpb, 0, 0, 0)))
    return pl.pallas_call(
        functools.partial(_proj_kernel, segs=segs, has_rope=has_rope, tm=tm, seq=seq),
        grid=(nt,), in_specs=in_specs, out_specs=out_specs, out_shape=out_shapes,
        compiler_params=_cparams(("parallel",)), name="proj",
    )(*args)


def _sb_tile(qs, kts, vts, u2, carry_ref, acc_ref, mask):
    ck = u2.shape[1]
    nch = kts[0].shape[0] // ck
    rows_h = qs[0].shape[0]
    rows = rows_h * len(qs)
    z = jnp.concatenate([_dot_nt(q, kt) for q, kt in zip(qs, kts)], axis=0) * ATTN_SCALE
    sp = jnp.log(1.0 + jnp.exp(-jnp.abs(z)))
    lb = jnp.minimum(z, 0.0) - sp
    l1m = lb - z
    if mask is not None:
        l1m = jnp.where(mask, l1m, 0.0)
    chunks = [l1m[:, c * ck:(c + 1) * ck] for c in range(nch)]
    st = jnp.concatenate(chunks, axis=0) if nch > 1 else l1m
    hi = st.astype(BF16)
    lo = (st - hi.astype(F32)).astype(BF16)
    rest = jnp.dot(jnp.concatenate([hi, lo], axis=1), u2, preferred_element_type=F32)
    carry = carry_ref[...]
    reps = ck // LANES
    a_parts = [None] * nch
    for c in reversed(range(nch)):
        rest_c = rest[c * rows:(c + 1) * rows]
        a_parts[c] = jnp.exp(lb[:, c * ck:(c + 1) * ck] + rest_c + jnp.tile(carry, (1, reps)))
        carry = carry + (rest_c[:, :1] + chunks[c][:, :1])
    a = jnp.concatenate(a_parts, axis=1) if nch > 1 else a_parts[0]
    if mask is not None:
        a = jnp.where(mask, a, 0.0)
    a = a.astype(BF16)
    for g, vt in enumerate(vts):
        sl = slice(g * rows_h, (g + 1) * rows_h)
        acc_ref[sl, :] += jnp.dot(a[sl], vt, preferred_element_type=F32)
    carry_ref[...] = carry


def _sb_prompt_kernel(q_ref, k_ref, v_ref, u_ref, o_ref, carry_ref, acc_ref, *, tq, n_grp):
    i = pl.program_id(2)
    u2 = u_ref[...]
    carry_ref[...] = jnp.zeros_like(carry_ref)
    acc_ref[...] = jnp.zeros_like(acc_ref)
    row = lax.broadcasted_iota(jnp.int32, (n_grp * tq, tq), 0)
    col = lax.broadcasted_iota(jnp.int32, (n_grp * tq, tq), 1)
    qs = [q_ref[0, g] for g in range(n_grp)]

    def tile(k0, mask):
        _sb_tile(qs, [k_ref[0, g, pl.ds(k0, tq), :] for g in range(n_grp)],
                 [v_ref[0, g, pl.ds(k0, tq), :] for g in range(n_grp)],
                 u2, carry_ref, acc_ref, mask)

    tile(pl.multiple_of(i * tq, tq), col < row % tq)

    def body(s, c):
        tile(pl.multiple_of((i - 1 - s) * tq, tq), None)
        return c

    lax.fori_loop(0, i, body, 0)
    for g in range(n_grp):
        o_ref[0, g] = acc_ref[g * tq:(g + 1) * tq, :]


def _suffix_matrix(n):
    u = (np.arange(n)[:, None] > np.arange(n)[None, :]).astype(np.float32)
    return jnp.asarray(np.concatenate([u, u], axis=0), dtype=BF16)


def _sb_prompt(q_bf, k_bf, v_bf, tq=256, n_grp=6):
    B, H, T, _ = q_bf.shape
    assert H % n_grp == 0
    q_spec = pl.BlockSpec((1, n_grp, tq, HEAD_DIM), lambda b, h, i: (b, h, i, 0))
    kv_spec = pl.BlockSpec((1, n_grp, T, HEAD_DIM), lambda b, h, i: (b, h, 0, 0))
    return pl.pallas_call(
        functools.partial(_sb_prompt_kernel, tq=tq, n_grp=n_grp),
        grid=(B, H // n_grp, T // tq),
        in_specs=[q_spec, kv_spec, kv_spec, pl.BlockSpec((2 * tq, tq), lambda b, h, i: (0, 0))],
        out_specs=q_spec,
        out_shape=jax.ShapeDtypeStruct((B, H, T, HEAD_DIM), F32),
        scratch_shapes=[pltpu.VMEM((n_grp * tq, LANES), F32),
                        pltpu.VMEM((n_grp * tq, HEAD_DIM), F32)],
        compiler_params=_cparams(("parallel", "parallel", "arbitrary")), name="sb_prompt",
    )(q_bf, k_bf, v_bf, _suffix_matrix(tq))


def _select_topk(gate, past):
    lane = lax.broadcasted_iota(jnp.int32, gate.shape, 1)
    g = jnp.where(past, gate, -jnp.inf)
    sel = jnp.zeros(gate.shape, F32)
    for _ in range(MOBA_TOPK):
        m = jnp.max(g, axis=1, keepdims=True)
        idx = jnp.min(jnp.where(g == m, lane, LANES), axis=1, keepdims=True)
        hit = lane == idx
        sel = jnp.where(hit & (m > -jnp.inf), 1.0, sel)
        g = jnp.where(hit, -jnp.inf, g)
    return sel


def _rank_select(gate, n_past):
    nb = gate.shape[0]
    blk = lax.broadcasted_iota(jnp.int32, gate.shape, 0)
    rank = jnp.zeros(gate.shape, F32)
    for m in range(nb):
        gm = gate[m:m + 1, :]
        beats = (gm > gate) | ((gm == gate) & (blk > m))
        rank = rank + jnp.where(beats & (n_past > m), 1.0, 0.0)
    return jnp.where((blk < n_past) & (rank < MOBA_TOPK), 1.0, 0.0)


def _moba_prompt_kernel(q_ref, k_ref, vt_ref, vt_own_ref, km_ref, o_ref, m_ref, l_ref, acc_ref,
                        *, tq, nb, n_heads):
    i = pl.program_id(1)
    hs = range(n_heads)
    qf = [q_ref[0, g] for g in hs]
    q = [x.astype(BF16) for x in qf]
    gate = jnp.concatenate(
        [_dot_nt(km_ref[0, g], qf[g], precision=lax.Precision.HIGHEST) for g in hs], axis=1)
    sel = _rank_select(gate, i)
    hcols = lambda g: slice(g * tq, (g + 1) * tq)

    def scores(keys_of):
        return jnp.concatenate([_dot_nt(keys_of(g), q[g]) for g in hs], axis=1) * ATTN_SCALE

    def weighted_values(p, vt_of):
        p = p.astype(BF16)
        return jnp.concatenate(
            [jnp.dot(vt_of(g), p[:, hcols(g)], preferred_element_type=F32) for g in hs], axis=1)

    key = lax.broadcasted_iota(jnp.int32, (tq, n_heads * tq), 0)
    qry = lax.broadcasted_iota(jnp.int32, (tq, n_heads * tq), 1) % tq
    d0 = pl.multiple_of(i * tq, tq)
    s = jnp.where(key <= qry, scores(lambda g: k_ref[0, g, pl.ds(d0, tq), :]), NEG_BIG)
    m = jnp.max(s, axis=0, keepdims=True)
    p = jnp.exp(s - m)
    m_ref[...] = m
    l_ref[...] = jnp.sum(p, axis=0, keepdims=True)
    acc_ref[...] = weighted_values(p, lambda g: vt_own_ref[0, g])

    for n in range(nb - 1):
        @pl.when(n < i)
        def _():
            keep = sel[n:n + 1, :] > 0.0
            s = jnp.where(keep, scores(lambda g: k_ref[0, g, n * tq:(n + 1) * tq, :]), NEG_BIG)
            m_old = m_ref[...]
            m_new = jnp.maximum(m_old, jnp.max(s, axis=0, keepdims=True))
            alpha = jnp.exp(m_old - m_new)
            p = jnp.where(keep, jnp.exp(s - m_new), 0.0)
            m_ref[...] = m_new
            l_ref[...] = alpha * l_ref[...] + jnp.sum(p, axis=0, keepdims=True)
            acc_ref[...] = alpha * acc_ref[...] + weighted_values(
                p, lambda g: vt_ref[0, g, :, n * tq:(n + 1) * tq])

    o = acc_ref[...] / l_ref[...]
    for g in hs:
        o_ref[0, g] = o[:, hcols(g)].T


def _moba_prompt(q_f32, k_bf, vt_bf, kmean):
    B, H, T, _ = q_f32.shape
    tq = MOBA_BLOCK
    nb = T // tq
    q_spec = pl.BlockSpec((1, H, tq, HEAD_DIM), lambda b, i: (b, 0, i, 0))
    return pl.pallas_call(
        functools.partial(_moba_prompt_kernel, tq=tq, nb=nb, n_heads=H),
        grid=(B, nb),
        in_specs=[q_spec,
                  pl.BlockSpec((1, H, T, HEAD_DIM), lambda b, i: (b, 0, 0, 0)),
                  pl.BlockSpec((1, H, HEAD_DIM, T), lambda b, i: (b, 0, 0, 0)),
                  pl.BlockSpec((1, H, HEAD_DIM, tq), lambda b, i: (b, 0, 0, i)),
                  pl.BlockSpec((1, H, nb, HEAD_DIM), lambda b, i: (b, 0, 0, 0))],
        out_specs=q_spec,
        out_shape=jax.ShapeDtypeStruct((B, H, T, HEAD_DIM), F32),
        scratch_shapes=[pltpu.VMEM((1, H * tq), F32), pltpu.VMEM((1, H * tq), F32),
                        pltpu.VMEM((HEAD_DIM, H * tq), F32)],
        compiler_params=_cparams(("parallel", "arbitrary")), name="moba_prompt",
    )(q_f32, k_bf, vt_bf, vt_bf, kmean)


def _mem_attn_kernel(q_ref, k_ref, v_ref, o_ref):
    s = _dot_nt(q_ref[0, 0].astype(BF16), k_ref[0].astype(BF16)) * ATTN_SCALE
    m = jnp.max(s, axis=1, keepdims=True)
    p = jnp.exp(s - m)
    l = jnp.sum(p, axis=1, keepdims=True)
    o = jnp.dot(p.astype(BF16), v_ref[0].astype(BF16), preferred_element_type=F32)
    o_ref[0, 0] = o / l


def _mem_attn(q, mk, mv, tq):
    B, H, T, _ = q.shape
    n_mem = mk.shape[1]
    q_spec = pl.BlockSpec((1, 1, tq, HEAD_DIM), lambda b, h, i: (b, h, i, 0))
    kv_spec = pl.BlockSpec((1, n_mem, HEAD_DIM), lambda b, h, i: (b, 0, h))
    return pl.pallas_call(
        _mem_attn_kernel,
        grid=(B, H, T // tq),
        in_specs=[q_spec, kv_spec, kv_spec],
        out_specs=q_spec,
        out_shape=jax.ShapeDtypeStruct((B, H, T, HEAD_DIM), F32),
        compiler_params=_cparams(("parallel", "parallel", "arbitrary")), name="mem_attn",
    )(q, mk, mv)


def _merge_kernel(x_ref, oa_ref, ob_ref, om_ref, ga_ref, gb_ref, gm_ref,
                  na_ref, nb_ref, nm_ref, w_ref, lg_ref, lb_ref, y_ref, *, alpha):
    def group(o_ref, g_ref, n_ref):
        o = jnp.concatenate([_heads_to_lanes(o_ref, (bb,)) for bb in range(o_ref.shape[0])],
                            axis=0)
        r = o * lax.rsqrt(jnp.mean(o * o, axis=-1, keepdims=True) + RMS_EPS) * n_ref[...]
        g = g_ref[...]
        return (r * (g * (1.0 / (1.0 + jnp.exp(-g))))).astype(BF16)

    mix = jnp.concatenate([group(oa_ref, ga_ref, na_ref), group(ob_ref, gb_ref, nb_ref),
                           group(om_ref, gm_ref, nm_ref)], axis=1)
    sub = jnp.dot(mix, w_ref[...], preferred_element_type=F32)
    h = alpha * x_ref[...] + sub
    mu = jnp.mean(h, axis=-1, keepdims=True)
    d = h - mu
    var = jnp.mean(d * d, axis=-1, keepdims=True)
    y_ref[...] = d * lax.rsqrt(var + LN_EPS) * lg_ref[...] + lb_ref[...]


def _merge(x2d, o_a, o_b, o_m, g_a, g_b, g_m, norm_a, norm_b, norm_m, w_out_bf, ln_g, ln_b,
           alpha, tm):
    M, D = x2d.shape
    seq = o_a.shape[2]
    tpb = max(seq // tm, 1)
    rpb = min(tm, seq)
    row = lambda a: pl.BlockSpec((tm, a.shape[1]), lambda i: (i, 0))
    head = lambda a: pl.BlockSpec((tm // rpb, a.shape[1], rpb, HEAD_DIM),
                                  lambda i: (i // tpb, 0, i % tpb, 0))
    full = lambda a: pl.BlockSpec(a.shape, lambda i: (0, 0))
    vecs = [v.reshape(1, -1) for v in (norm_a, norm_b, norm_m, ln_g, ln_b)]
    args = [x2d, o_a, o_b, o_m, g_a, g_b, g_m, *vecs[:3], w_out_bf, *vecs[3:]]
    in_specs = ([row(x2d)] + [head(a) for a in (o_a, o_b, o_m)] + [row(a) for a in (g_a, g_b, g_m)]
                + [full(a) for a in args[7:]])
    return pl.pallas_call(
        functools.partial(_merge_kernel, alpha=alpha),
        grid=(M // tm,), in_specs=in_specs,
        out_specs=pl.BlockSpec((tm, D), lambda i: (i, 0)),
        out_shape=jax.ShapeDtypeStruct((M, D), F32),
        compiler_params=_cparams(("parallel",)), name="merge",
    )(*args)


def _block_diag_queries(q):
    B, H, Tq, d = q.shape
    eye = jnp.eye(H, dtype=q.dtype)
    return (q[:, :, :, None, :] * eye[None, :, None, :, None]).reshape(B, H * Tq, H * d)


def _new_token_rows(t):
    B, H, Tq, d = t.shape
    rows = jnp.transpose(t, (0, 2, 1, 3)).reshape(B, Tq, H * d)
    return jnp.pad(rows, ((0, 0), (0, LANES - Tq), (0, 0)))


def _page_rows(page_refs, dtype):
    return jnp.concatenate([_heads_to_lanes(r).astype(dtype) for r in page_refs], axis=0)


def _take_block_diag(acc, o_ref, n_heads, tq):
    for h in range(n_heads):
        o_ref[0, h] = acc[h * tq:(h + 1) * tq, h * HEAD_DIM:(h + 1) * HEAD_DIM]


def _sb_sample_kernel(pt_ref, q_ref, *rest, n_heads, tq, n_steps, n_pg):
    k_refs, v_refs = rest[:n_pg], rest[n_pg:2 * n_pg]
    kn_ref, vn_ref, u_ref, un_ref, o_ref, carry_ref, acc_ref = rest[2 * n_pg:]
    p = pl.program_id(1)
    q = q_ref[0]
    rows = n_heads * tq

    @pl.when(p == 0)
    def _():
        carry_ref[...] = jnp.zeros_like(carry_ref)
        acc_ref[...] = jnp.zeros_like(acc_ref)
        row = lax.broadcasted_iota(jnp.int32, (rows, LANES), 0)
        col = lax.broadcasted_iota(jnp.int32, (rows, LANES), 1)
        _sb_tile([q], [kn_ref[0].astype(BF16)], [vn_ref[0].astype(BF16)], un_ref[...],
                 carry_ref, acc_ref, col < row % tq)

    _sb_tile([q], [_page_rows(k_refs, BF16)], [_page_rows(v_refs, BF16)], u_ref[...],
             carry_ref, acc_ref, None)

    @pl.when(p == n_steps - 1)
    def _():
        _take_block_diag(acc_ref[...], o_ref, n_heads, tq)


def _sb_sample(q_bd_bf, k_pool, v_pool, k_new_rows, v_new_rows, page_table, tq):
    B, rows, W = q_bd_bf.shape
    _, H, page, _ = k_pool.shape
    n_pg = SAMPLE_PAGES_PER_STEP
    n_steps = page_table.shape[1] // n_pg
    assert n_steps * n_pg == page_table.shape[1]

    def pg(j):
        return pl.BlockSpec((None, H, page, HEAD_DIM),
                            lambda b, p, pt: (pt[b, n_pg * (n_steps - 1 - p) + j], 0, 0, 0))

    per_b = lambda a: pl.BlockSpec((1,) + a.shape[1:], lambda b, p, pt: (b, 0, 0))
    u2, un2 = _suffix_matrix(2 * page), _suffix_matrix(LANES)
    const = lambda a: pl.BlockSpec(a.shape, lambda b, p, pt: (0, 0))
    pages = [pg(j) for j in range(n_pg)]
    grid_spec = pltpu.PrefetchScalarGridSpec(
        num_scalar_prefetch=1, grid=(B, n_steps),
        in_specs=[per_b(q_bd_bf), *pages, *pages, per_b(k_new_rows), per_b(v_new_rows),
                  const(u2), const(un2)],
        out_specs=pl.BlockSpec((1, H, tq, HEAD_DIM), lambda b, p, pt: (b, 0, 0, 0)),
        scratch_shapes=[pltpu.VMEM((rows, LANES), F32), pltpu.VMEM((rows, W), F32)])
    return pl.pallas_call(
        functools.partial(_sb_sample_kernel, n_heads=H, tq=tq, n_steps=n_steps, n_pg=n_pg),
        grid_spec=grid_spec, out_shape=jax.ShapeDtypeStruct((B, H, tq, HEAD_DIM), F32),
        compiler_params=_cparams(("parallel", "arbitrary")), name="sb_sample",
    )(page_table, q_bd_bf, *([k_pool] * n_pg), *([v_pool] * n_pg), k_new_rows, v_new_rows,
      u2, un2)


def _moba_sample_kernel(pt_ref, qf_ref, *rest, n_heads, tq, n_steps, n_pg):
    k_refs, v_refs = rest[:n_pg], rest[n_pg:2 * n_pg]
    kn_ref, vn_ref, o_ref, s_ref, p_ref, km_ref, pn_ref, l_ref, acc_ref = rest[2 * n_pg:]
    ph = pl.program_id(1)
    n = pl.program_id(2)
    rows = n_heads * tq
    blk = MOBA_BLOCK
    bps = n_pg // 2
    nb = n_steps * bps
    blk_cols = lambda j: (j // bps, slice(None), slice((j % bps) * blk, (j % bps + 1) * blk))

    @pl.when((ph == 0) & (n == 0))
    def _():
        km_ref[...] = jnp.zeros_like(km_ref)

    @pl.when(ph == 0)
    def _():
        pages = [_heads_to_lanes(r) for r in k_refs]
        for c in range(bps):
            km_ref[pl.ds(n * bps + c, 1), :] = (
                jnp.sum(pages[2 * c], axis=0, keepdims=True)
                + jnp.sum(pages[2 * c + 1], axis=0, keepdims=True)) * (1.0 / blk)
        kc = jnp.concatenate([pg.astype(BF16) for pg in pages], axis=0)
        s_ref[n] = _dot_nt(qf_ref[0].astype(BF16), kc) * ATTN_SCALE

    @pl.when((ph == 0) & (n == n_steps - 1))
    def _():
        qf = qf_ref[0]
        gate = _dot_nt(qf, km_ref[...], precision=lax.Precision.HIGHEST)
        lane = lax.broadcasted_iota(jnp.int32, (rows, LANES), 1)
        sel = _select_topk(gate, lane < nb)
        row = lax.broadcasted_iota(jnp.int32, (rows, LANES), 0)
        s_new = jnp.where(lane <= row % tq,
                          _dot_nt(qf.astype(BF16), kn_ref[0].astype(BF16)) * ATTN_SCALE, NEG_BIG)
        m = jnp.max(s_new, axis=1, keepdims=True)
        for j in range(nb):
            sj = jnp.where(sel[:, j:j + 1] > 0.0, s_ref[blk_cols(j)], NEG_BIG)
            m = jnp.maximum(m, jnp.max(sj, axis=1, keepdims=True))
        p_new = jnp.exp(s_new - m)
        l = jnp.sum(p_new, axis=1, keepdims=True)
        pn_ref[...] = p_new.astype(BF16)
        for j in range(nb):
            keep = sel[:, j:j + 1] > 0.0
            pj = jnp.where(keep, jnp.exp(jnp.where(keep, s_ref[blk_cols(j)], NEG_BIG) - m), 0.0)
            l = l + jnp.sum(pj, axis=1, keepdims=True)
            p_ref[blk_cols(j)] = pj.astype(BF16)
        l_ref[...] = l
        acc_ref[...] = jnp.dot(pn_ref[...], vn_ref[0].astype(BF16), preferred_element_type=F32)

    @pl.when(ph == 1)
    def _():
        acc_ref[...] += jnp.dot(p_ref[n], _page_rows(v_refs, BF16), preferred_element_type=F32)

    @pl.when((ph == 1) & (n == n_steps - 1))
    def _():
        _take_block_diag(acc_ref[...] / l_ref[...], o_ref, n_heads, tq)


def _moba_sample(q_bd_f32, k_pool, v_pool, k_new_rows, v_new_rows, page_table, tq):
    B, rows, W = q_bd_f32.shape
    _, H, page, _ = k_pool.shape
    assert 2 * page == MOBA_BLOCK
    n_pg = SAMPLE_PAGES_PER_STEP
    n_steps = page_table.shape[1] // n_pg
    assert n_steps * n_pg == page_table.shape[1] and n_pg % 2 == 0
    assert page_table.shape[1] // 2 <= LANES

    def kpg(j):
        return pl.BlockSpec(
            (None, H, page, HEAD_DIM),
            lambda b, ph, n, pt: (pt[b, n_pg * jnp.where(ph == 0, n, n_steps - 1) + j], 0, 0, 0))

    def vpg(j):
        return pl.BlockSpec(
            (None, H, page, HEAD_DIM),
            lambda b, ph, n, pt: (pt[b, n_pg * jnp.where(ph == 0, 0, n) + j], 0, 0, 0))

    per_b = lambda a: pl.BlockSpec((1,) + a.shape[1:], lambda b, ph, n, pt: (b, 0, 0))
    step_keys = n_pg * page
    grid_spec = pltpu.PrefetchScalarGridSpec(
        num_scalar_prefetch=1, grid=(B, 2, n_steps),
        in_specs=[per_b(q_bd_f32), *[kpg(j) for j in range(n_pg)],
                  *[vpg(j) for j in range(n_pg)], per_b(k_new_rows), per_b(v_new_rows)],
        out_specs=pl.BlockSpec((1, H, tq, HEAD_DIM), lambda b, ph, n, pt: (b, 0, 0, 0)),
        scratch_shapes=[pltpu.VMEM((n_steps, rows, step_keys), F32),
                        pltpu.VMEM((n_steps, rows, step_keys), BF16),
                        pltpu.VMEM((LANES, W), F32),
                        pltpu.VMEM((rows, LANES), BF16),
                        pltpu.VMEM((rows, 1), F32),
                        pltpu.VMEM((rows, W), F32)])
    return pl.pallas_call(
        functools.partial(_moba_sample_kernel, n_heads=H, tq=tq, n_steps=n_steps, n_pg=n_pg),
        grid_spec=grid_spec, out_shape=jax.ShapeDtypeStruct((B, H, tq, HEAD_DIM), F32),
        compiler_params=_cparams(("parallel", "arbitrary", "arbitrary")), name="moba_sample",
    )(page_table, q_bd_f32, *([k_pool] * n_pg), *([v_pool] * n_pg), k_new_rows, v_new_rows)


def _rope_tables(pos):
    inv = ROPE_THETA ** (-jnp.arange(0, HEAD_DIM, 2, dtype=F32) / HEAD_DIM)
    ang = pos.astype(F32)[:, None] * inv[None, :]
    cos, sin = jnp.cos(ang), jnp.sin(ang)
    return jnp.concatenate([cos, cos], -1), jnp.concatenate([-sin, sin], -1)


def _project_all(x2d, w_in_bf, w_sb, w_moba, w_mem, tm, seq, rope_tabs, prompt):
    assert w_sb == w_moba and (8 * w_sb) % (2 * w_mem) == 0
    grp = 4 * w_sb
    kv = (F32, BF16) if prompt else (F32,)
    seg = lambda k, width, rope=False, rows=(), heads=(), headt=(), km=False: (
        k * width, width, rope, rows, heads, headt, km)
    a = _project(x2d, w_in_bf, 0, grp,
                 (seg(0, w_sb, heads=(BF16,) if prompt else (F32,)), seg(1, w_sb, heads=kv),
                  seg(2, w_sb, heads=kv), seg(3, w_sb, rows=(F32,))), tm, seq)
    b = _project(x2d, w_in_bf, 1, grp,
                 (seg(0, w_moba, rope=True, heads=(F32,)),
                  seg(1, w_moba, rope=True, heads=kv, km=prompt),
                  seg(2, w_moba, heads=(F32,), headt=(BF16,) if prompt else ()),
                  seg(3, w_moba, rows=(F32,))), tm, seq, rope_tabs=rope_tabs)
    qm, gm = _project(x2d, w_in_bf, (2 * grp) // (2 * w_mem), 2 * w_mem,
                      (seg(0, w_mem, heads=(F32,)), seg(1, w_mem, rows=(F32,))), tm, seq)
    if prompt:
        names_a = ("qa", "ka", "ka_bf", "va", "va_bf", "ga")
        names_b = ("qb", "kb", "kb_bf", "kmean", "vb", "vbt_bf", "gb")
    else:
        names_a = ("qa", "ka", "va", "ga")
        names_b = ("qb", "kb", "vb", "gb")
    out = dict(zip(names_a, a))
    out.update(zip(names_b, b))
    out.update(qm=qm, gm=gm)
    return out


def kernel(x_prompt, x_sample, cache_sb_k, cache_sb_v, cache_moba_k, cache_moba_v,
           cache_mem_k, cache_mem_v, page_table, mem_prompt,
           w_in, w_mem_k, w_mem_v, norm_a, norm_b, norm_m, w_out, ln_g, ln_b):
    depth = w_in.shape[0]
    B, T, D = x_prompt.shape
    Bs, Ts, _ = x_sample.shape
    h_sb, h_moba, h_mem = cache_sb_k.shape[3], cache_moba_k.shape[3], cache_mem_k.shape[3]
    w_sb, w_moba, w_mem = h_sb * HEAD_DIM, h_moba * HEAD_DIM, h_mem * HEAD_DIM
    n_mem = mem_prompt.shape[1]
    page = cache_sb_k.shape[2]
    past_len = page_table.shape[1] * page
    assert past_len % MOBA_BLOCK == 0 and Ts <= LANES and T % MOBA_BLOCK == 0
    alpha = (2.0 * depth) ** 0.25

    tm = 512
    rope_p = _rope_tables(jnp.arange(T, dtype=jnp.int32))
    rope_s = tuple(jnp.tile(t, (Bs, 1)) for t in
                   _rope_tables(past_len + jnp.arange(Ts, dtype=jnp.int32)))

    y_p = x_prompt.reshape(B * T, D)
    y_s = x_sample.reshape(Bs * Ts, D)
    outs = [[] for _ in range(10)]
    for l in range(depth):
        w_in_bf = w_in[l].astype(BF16)
        w_out_bf = w_out[l].astype(BF16)
        w_memkv_bf = jnp.concatenate([w_mem_k[l], w_mem_v[l]], axis=1).astype(BF16)

        pp = _project_all(y_p, w_in_bf, w_sb, w_moba, w_mem, tm, T, rope_p, True)
        o_a = _sb_prompt(pp["qa"], pp["ka_bf"], pp["va_bf"])
        kmean = jnp.transpose(pp["kmean"], (0, 2, 1, 3, 4)).reshape(
            B, h_moba, T // MOBA_BLOCK, HEAD_DIM)
        o_b = _moba_prompt(pp["qb"], pp["kb_bf"], pp["vbt_bf"], kmean)
        mk, mv = _project(
            mem_prompt.reshape(B * n_mem, D), w_memkv_bf, 0, 2 * w_mem,
            ((0, w_mem, False, (F32,), (), (), False),
             (w_mem, w_mem, False, (F32,), (), (), False)),
            256, n_mem)
        o_m = _mem_attn(pp["qm"], mk.reshape(B, n_mem, w_mem), mv.reshape(B, n_mem, w_mem), 512)
        y_p_new = _merge(y_p, o_a, o_b, o_m, pp["ga"], pp["gb"], pp["gm"],
                         norm_a[l], norm_b[l], norm_m[l], w_out_bf, ln_g[l], ln_b[l], alpha, 256)

        ps = _project_all(y_s, w_in_bf, w_sb, w_moba, w_mem, Bs * Ts, Ts, rope_s, False)
        pool = lambda c: jnp.transpose(c[l], (0, 2, 1, 3))
        so_a = _sb_sample(_block_diag_queries(ps["qa"]).astype(BF16), pool(cache_sb_k),
                          pool(cache_sb_v), _new_token_rows(ps["ka"]), _new_token_rows(ps["va"]),
                          page_table, Ts)
        so_b = _moba_sample(_block_diag_queries(ps["qb"]), pool(cache_moba_k),
                            pool(cache_moba_v), _new_token_rows(ps["kb"]),
                            _new_token_rows(ps["vb"]), page_table, Ts)
        so_m = _mem_attn(ps["qm"], cache_mem_k[l].reshape(Bs, n_mem, w_mem),
                         cache_mem_v[l].reshape(Bs, n_mem, w_mem), Ts)
        y_s_new = _merge(y_s, so_a, so_b, so_m, ps["ga"], ps["gb"], ps["gm"],
                         norm_a[l], norm_b[l], norm_m[l], w_out_bf, ln_g[l], ln_b[l], alpha,
                         Bs * Ts)

        tok_major = lambda a: jnp.transpose(a, (0, 2, 1, 3))
        new = [tok_major(pp["ka"]), tok_major(pp["va"]), tok_major(pp["kb"]), tok_major(pp["vb"]),
               mk.reshape(B, n_mem, h_mem, HEAD_DIM), mv.reshape(B, n_mem, h_mem, HEAD_DIM),
               tok_major(ps["ka"]), tok_major(ps["va"]), tok_major(ps["kb"]), tok_major(ps["vb"])]
        for lst, a in zip(outs, new):
            lst.append(a)
        y_p, y_s = y_p_new, y_s_new

    return (y_p.reshape(B, T, D), y_s.reshape(Bs, Ts, D), *[jnp.stack(o) for o in outs])
```

```python
import functools

import jax
import jax.numpy as jnp
import numpy as np
from jax import lax
from jax.experimental import pallas as pl
from jax.experimental.pallas import tpu as pltpu

F32 = jnp.float32
BF16 = jnp.bfloat16

HEAD_DIM = 128
MOBA_BLOCK = 256
MOBA_TOPK = 3
ROPE_THETA = 10000.0
LN_EPS = 1e-5
RMS_EPS = 1e-6
ATTN_SCALE = HEAD_DIM ** -0.5
NEG_BIG = -1e30
LANES = 128
VMEM_LIMIT = 56 * 1024 * 1024
SAMPLE_PAGES_PER_STEP = 8


def _cparams(sem):
    return pltpu.CompilerParams(dimension_semantics=sem, vmem_limit_bytes=VMEM_LIMIT)


def _dot_nt(a, b, precision=None):
    return lax.dot_general(a, b, (((1,), (1,)), ((), ())), precision=precision,
                           preferred_element_type=F32)


def _heads_to_lanes(ref, idx=()):
    n_heads = ref.shape[len(idx)]
    return jnp.concatenate([ref[idx + (h,)] for h in range(n_heads)], axis=1)


def _proj_kernel(*refs, segs, has_rope, tm, seq):
    x_ref, w_ref = refs[0], refs[1]
    pos = 2
    if has_rope:
        cos_ref, sin_ref = refs[2], refs[3]
        pos = 4
    outs = refs[pos:]
    x = x_ref[...].astype(BF16)
    rpb = min(tm, seq)
    o = 0
    for c0, width, rope, row_dtypes, head_dtypes, headt_dtypes, want_kmean in segs:
        r = jnp.dot(x, w_ref[:, c0:c0 + width], preferred_element_type=F32)
        n_heads = width // HEAD_DIM
        heads = [r[:, h * HEAD_DIM:(h + 1) * HEAD_DIM] for h in range(n_heads)]
        if rope:
            cos = cos_ref[...]
            sin = sin_ref[...]
            heads = [xh * cos + pltpu.roll(xh, HEAD_DIM // 2, axis=1) * sin for xh in heads]
            r = jnp.concatenate(heads, axis=1)
        for dt in row_dtypes:
            outs[o][...] = r.astype(dt)
            o += 1
        for dt in head_dtypes:
            for bb in range(tm // rpb):
                for h in range(n_heads):
                    outs[o][bb, h] = heads[h][bb * rpb:(bb + 1) * rpb].astype(dt)
            o += 1
        for dt in headt_dtypes:
            for h in range(n_heads):
                outs[o][0, h] = heads[h].T.astype(dt)
            o += 1
        if want_kmean:
            nblk = tm // MOBA_BLOCK
            for h in range(n_heads):
                outs[o][0, 0, h] = jnp.sum(heads[h].reshape(nblk, MOBA_BLOCK, HEAD_DIM),
                                           axis=1) * (1.0 / MOBA_BLOCK)
            o += 1


def _project(x2d, w_bf, col_block, col_width, segs, tm, seq, rope_tabs=None):
    M, D = x2d.shape
    nt = M // tm
    n_batch = M // seq
    tpb = max(seq // tm, 1)
    rpb = min(tm, seq)
    has_rope = rope_tabs is not None
    in_specs = [pl.BlockSpec((tm, D), lambda i: (i, 0)),
                pl.BlockSpec((D, col_width), lambda i: (0, col_block))]
    args = [x2d, w_bf]
    if has_rope:
        in_specs += [pl.BlockSpec((tm, HEAD_DIM), lambda i: (i % tpb, 0))] * 2
        args += list(rope_tabs)
    out_shapes, out_specs = [], []
    for c0, width, rope, row_dtypes, head_dtypes, headt_dtypes, want_kmean in segs:
        n_heads = width // HEAD_DIM
        for dt in row_dtypes:
            out_shapes.append(jax.ShapeDtypeStruct((M, width), dt))
            out_specs.append(pl.BlockSpec((tm, width), lambda i: (i, 0)))
        for dt in head_dtypes:
            out_shapes.append(jax.ShapeDtypeStruct((n_batch, n_heads, seq, HEAD_DIM), dt))
            out_specs.append(pl.BlockSpec((tm // rpb, n_heads, rpb, HEAD_DIM),
                                          lambda i: (i // tpb, 0, i % tpb, 0)))
        for dt in headt_dtypes:
            assert tm <= seq
            out_shapes.append(jax.ShapeDtypeStruct((n_batch, n_heads, HEAD_DIM, seq), dt))
            out_specs.append(pl.BlockSpec((1, n_heads, HEAD_DIM, tm),
                                          lambda i: (i // tpb, 0, 0, i % tpb)))
        if want_kmean:
            nblk = tm // MOBA_BLOCK
            out_shapes.append(jax.ShapeDtypeStruct((n_batch, tpb, n_heads, nblk, HEAD_DIM), F32))
            out_specs.append(pl.BlockSpec((1, 1, n_heads, nblk, HEAD_DIM),
                                          lambda i: (i // tpb, i % tpb, 0, 0, 0)))
    return pl.pallas_call(
        functools.partial(_proj_kernel, segs=segs, has_rope=has_rope, tm=tm, seq=seq),
        grid=(nt,), in_specs=in_specs, out_specs=out_specs, out_shape=out_shapes,
        compiler_params=_cparams(("parallel",)), name="proj",
    )(*args)


def _sb_tile(qs, kts, vts, u2, carry_ref, acc_ref, mask):
    ck = u2.shape[1]
    nch = kts[0].shape[0] // ck
    rows_h = qs[0].shape[0]
    rows = rows_h * len(qs)
    z = jnp.concatenate([_dot_nt(q, kt) for q, kt in zip(qs, kts)], axis=0) * ATTN_SCALE
    sp = jnp.log(1.0 + jnp.exp(-jnp.abs(z)))
    lb = jnp.minimum(z, 0.0) - sp
    l1m = lb - z
    if mask is not None:
        l1m = jnp.where(mask, l1m, 0.0)
    chunks = [l1m[:, c * ck:(c + 1) * ck] for c in range(nch)]
    st = jnp.concatenate(chunks, axis=0) if nch > 1 else l1m
    rest = jnp.dot(st.astype(BF16), u2, preferred_element_type=F32)
    carry = carry_ref[...]
    reps = ck // LANES
    a_parts = [None] * nch
    for c in reversed(range(nch)):
        rest_c = rest[c * rows:(c + 1) * rows]
        a_parts[c] = jnp.exp(lb[:, c * ck:(c + 1) * ck] + rest_c + jnp.tile(carry, (1, reps)))
        carry = carry + (rest_c[:, :1] + chunks[c][:, :1])
    a = jnp.concatenate(a_parts, axis=1) if nch > 1 else a_parts[0]
    if mask is not None:
        a = jnp.where(mask, a, 0.0)
    a = a.astype(BF16)
    for g, vt in enumerate(vts):
        sl = slice(g * rows_h, (g + 1) * rows_h)
        acc_ref[sl, :] += jnp.dot(a[sl], vt, preferred_element_type=F32)
    carry_ref[...] = carry


def _sb_prompt_kernel(q_ref, k_ref, v_ref, u_ref, o_ref, carry_ref, acc_ref, *, tq, n_grp):
    i = pl.program_id(2)
    u2 = u_ref[...]
    carry_ref[...] = jnp.zeros_like(carry_ref)
    acc_ref[...] = jnp.zeros_like(acc_ref)
    row = lax.broadcasted_iota(jnp.int32, (n_grp * tq, tq), 0)
    col = lax.broadcasted_iota(jnp.int32, (n_grp * tq, tq), 1)
    qs = [q_ref[0, g] for g in range(n_grp)]

    def tile(k0, mask):
        _sb_tile(qs, [k_ref[0, g, pl.ds(k0, tq), :] for g in range(n_grp)],
                 [v_ref[0, g, pl.ds(k0, tq), :] for g in range(n_grp)],
                 u2, carry_ref, acc_ref, mask)

    tile(pl.multiple_of(i * tq, tq), col < row % tq)

    def body(s, c):
        tile(pl.multiple_of((i - 1 - s) * tq, tq), None)
        return c

    lax.fori_loop(0, i, body, 0)
    for g in range(n_grp):
        o_ref[0, g] = acc_ref[g * tq:(g + 1) * tq, :]


def _suffix_matrix(n):
    u = (np.arange(n)[:, None] > np.arange(n)[None, :]).astype(np.float32)
    return jnp.asarray(u, dtype=BF16)


def _sb_prompt(q_bf, k_bf, v_bf, tq=256, n_grp=6):
    B, H, T, _ = q_bf.shape
    assert H % n_grp == 0
    q_spec = pl.BlockSpec((1, n_grp, tq, HEAD_DIM), lambda b, h, i: (b, h, i, 0))
    kv_spec = pl.BlockSpec((1, n_grp, T, HEAD_DIM), lambda b, h, i: (b, h, 0, 0))
    return pl.pallas_call(
        functools.partial(_sb_prompt_kernel, tq=tq, n_grp=n_grp),
        grid=(B, H // n_grp, T // tq),
        in_specs=[q_spec, kv_spec, kv_spec, pl.BlockSpec((tq, tq), lambda b, h, i: (0, 0))],
        out_specs=q_spec,
        out_shape=jax.ShapeDtypeStruct((B, H, T, HEAD_DIM), F32),
        scratch_shapes=[pltpu.VMEM((n_grp * tq, LANES), F32),
                        pltpu.VMEM((n_grp * tq, HEAD_DIM), F32)],
        compiler_params=_cparams(("parallel", "parallel", "arbitrary")), name="sb_prompt",
    )(q_bf, k_bf, v_bf, _suffix_matrix(tq))


def _select_topk(gate, past):
    lane = lax.broadcasted_iota(jnp.int32, gate.shape, 1)
    g = jnp.where(past, gate, -jnp.inf)
    sel = jnp.zeros(gate.shape, F32)
    for _ in range(MOBA_TOPK):
        m = jnp.max(g, axis=1, keepdims=True)
        idx = jnp.min(jnp.where(g == m, lane, LANES), axis=1, keepdims=True)
        hit = lane == idx
        sel = jnp.where(hit & (m > -jnp.inf), 1.0, sel)
        g = jnp.where(hit, -jnp.inf, g)
    return sel


def _rank_select(gate, n_past):
    nb = gate.shape[0]
    blk = lax.broadcasted_iota(jnp.int32, gate.shape, 0)
    rank = jnp.zeros(gate.shape, F32)
    for m in range(nb):
        gm = gate[m:m + 1, :]
        beats = (gm > gate) | ((gm == gate) & (blk > m))
        rank = rank + jnp.where(beats & (n_past > m), 1.0, 0.0)
    return jnp.where((blk < n_past) & (rank < MOBA_TOPK), 1.0, 0.0)


def _moba_prompt_kernel(q_ref, k_ref, vt_ref, vt_own_ref, km_ref, o_ref, m_ref, l_ref, acc_ref,
                        *, tq, nb, n_heads):
    i = pl.program_id(1)
    hs = range(n_heads)
    qf = [q_ref[0, g] for g in hs]
    q = [x.astype(BF16) for x in qf]
    gate = jnp.concatenate(
        [_dot_nt(km_ref[0, g], qf[g], precision=lax.Precision.HIGHEST) for g in hs], axis=1)
    sel = _rank_select(gate, i)
    hcols = lambda g: slice(g * tq, (g + 1) * tq)

    def scores(keys_of):
        return jnp.concatenate([_dot_nt(keys_of(g), q[g]) for g in hs], axis=1) * ATTN_SCALE

    def weighted_values(p, vt_of):
        p = p.astype(BF16)
        return jnp.concatenate(
            [jnp.dot(vt_of(g), p[:, hcols(g)], preferred_element_type=F32) for g in hs], axis=1)

    key = lax.broadcasted_iota(jnp.int32, (tq, n_heads * tq), 0)
    qry = lax.broadcasted_iota(jnp.int32, (tq, n_heads * tq), 1) % tq
    d0 = pl.multiple_of(i * tq, tq)
    s = jnp.where(key <= qry, scores(lambda g: k_ref[0, g, pl.ds(d0, tq), :]), NEG_BIG)
    m = jnp.max(s, axis=0, keepdims=True)
    p = jnp.exp(s - m)
    m_ref[...] = m
    l_ref[...] = jnp.sum(p, axis=0, keepdims=True)
    acc_ref[...] = weighted_values(p, lambda g: vt_own_ref[0, g])

    for n in range(nb - 1):
        @pl.when(n < i)
        def _():
            keep = sel[n:n + 1, :] > 0.0
            s = jnp.where(keep, scores(lambda g: k_ref[0, g, n * tq:(n + 1) * tq, :]), NEG_BIG)
            m_old = m_ref[...]
            m_new = jnp.maximum(m_old, jnp.max(s, axis=0, keepdims=True))
            alpha = jnp.exp(m_old - m_new)
            p = jnp.where(keep, jnp.exp(s - m_new), 0.0)
            m_ref[...] = m_new
            l_ref[...] = alpha * l_ref[...] + jnp.sum(p, axis=0, keepdims=True)
            acc_ref[...] = alpha * acc_ref[...] + weighted_values(
                p, lambda g: vt_ref[0, g, :, n * tq:(n + 1) * tq])

    o = acc_ref[...] / l_ref[...]
    for g in hs:
        o_ref[0, g] = o[:, hcols(g)].T


def _moba_prompt(q_f32, k_bf, vt_bf, kmean):
    B, H, T, _ = q_f32.shape
    tq = MOBA_BLOCK
    nb = T // tq
    q_spec = pl.BlockSpec((1, H, tq, HEAD_DIM), lambda b, i: (b, 0, i, 0))
    return pl.pallas_call(
        functools.partial(_moba_prompt_kernel, tq=tq, nb=nb, n_heads=H),
        grid=(B, nb),
        in_specs=[q_spec,
                  pl.BlockSpec((1, H, T, HEAD_DIM), lambda b, i: (b, 0, 0, 0)),
                  pl.BlockSpec((1, H, HEAD_DIM, T), lambda b, i: (b, 0, 0, 0)),
                  pl.BlockSpec((1, H, HEAD_DIM, tq), lambda b, i: (b, 0, 0, i)),
                  pl.BlockSpec((1, H, nb, HEAD_DIM), lambda b, i: (b, 0, 0, 0))],
        out_specs=q_spec,
        out_shape=jax.ShapeDtypeStruct((B, H, T, HEAD_DIM), F32),
        scratch_shapes=[pltpu.VMEM((1, H * tq), F32), pltpu.VMEM((1, H * tq), F32),
                        pltpu.VMEM((HEAD_DIM, H * tq), F32)],
        compiler_params=_cparams(("parallel", "arbitrary")), name="moba_prompt",
    )(q_f32, k_bf, vt_bf, vt_bf, kmean)


def _mem_attn_kernel(q_ref, k_ref, v_ref, o_ref, *, n_heads):
    for h in range(n_heads):
        cols = slice(h * HEAD_DIM, (h + 1) * HEAD_DIM)
        s = _dot_nt(q_ref[0, h].astype(BF16), k_ref[0, :, cols].astype(BF16)) * ATTN_SCALE
        m = jnp.max(s, axis=1, keepdims=True)
        p = jnp.exp(s - m)
        l = jnp.sum(p, axis=1, keepdims=True)
        o = jnp.dot(p.astype(BF16), v_ref[0, :, cols].astype(BF16), preferred_element_type=F32)
        o_ref[0, h] = o / l


def _mem_attn(q, mk, mv, tq):
    B, H, T, _ = q.shape
    n_mem = mk.shape[1]
    q_spec = pl.BlockSpec((1, H, tq, HEAD_DIM), lambda b, i: (b, 0, i, 0))
    kv_spec = pl.BlockSpec((1, n_mem, H * HEAD_DIM), lambda b, i: (b, 0, 0))
    return pl.pallas_call(
        functools.partial(_mem_attn_kernel, n_heads=H),
        grid=(B, T // tq),
        in_specs=[q_spec, kv_spec, kv_spec],
        out_specs=q_spec,
        out_shape=jax.ShapeDtypeStruct((B, H, T, HEAD_DIM), F32),
        compiler_params=_cparams(("parallel", "arbitrary")), name="mem_attn",
    )(q, mk, mv)


def _merge_kernel(x_ref, oa_ref, ob_ref, om_ref, ga_ref, gb_ref, gm_ref,
                  na_ref, nb_ref, nm_ref, w_ref, lg_ref, lb_ref, y_ref, *, alpha):
    def group(o_ref, g_ref, n_ref):
        o = jnp.concatenate([_heads_to_lanes(o_ref, (bb,)) for bb in range(o_ref.shape[0])],
                            axis=0)
        r = o * lax.rsqrt(jnp.mean(o * o, axis=-1, keepdims=True) + RMS_EPS) * n_ref[...]
        g = g_ref[...]
        return (r * (g * (1.0 / (1.0 + jnp.exp(-g))))).astype(BF16)

    mix = jnp.concatenate([group(oa_ref, ga_ref, na_ref), group(ob_ref, gb_ref, nb_ref),
                           group(om_ref, gm_ref, nm_ref)], axis=1)
    sub = jnp.dot(mix, w_ref[...], preferred_element_type=F32)
    h = alpha * x_ref[...] + sub
    mu = jnp.mean(h, axis=-1, keepdims=True)
    d = h - mu
    var = jnp.mean(d * d, axis=-1, keepdims=True)
    y_ref[...] = d * lax.rsqrt(var + LN_EPS) * lg_ref[...] + lb_ref[...]


def _merge(x2d, o_a, o_b, o_m, g_a, g_b, g_m, norm_a, norm_b, norm_m, w_out_bf, ln_g, ln_b,
           alpha, tm):
    M, D = x2d.shape
    seq = o_a.shape[2]
    tpb = max(seq // tm, 1)
    rpb = min(tm, seq)
    row = lambda a: pl.BlockSpec((tm, a.shape[1]), lambda i: (i, 0))
    head = lambda a: pl.BlockSpec((tm // rpb, a.shape[1], rpb, HEAD_DIM),
                                  lambda i: (i // tpb, 0, i % tpb, 0))
    full = lambda a: pl.BlockSpec(a.shape, lambda i: (0, 0))
    vecs = [v.reshape(1, -1) for v in (norm_a, norm_b, norm_m, ln_g, ln_b)]
    args = [x2d, o_a, o_b, o_m, g_a, g_b, g_m, *vecs[:3], w_out_bf, *vecs[3:]]
    in_specs = ([row(x2d)] + [head(a) for a in (o_a, o_b, o_m)] + [row(a) for a in (g_a, g_b, g_m)]
                + [full(a) for a in args[7:]])
    return pl.pallas_call(
        functools.partial(_merge_kernel, alpha=alpha),
        grid=(M // tm,), in_specs=in_specs,
        out_specs=pl.BlockSpec((tm, D), lambda i: (i, 0)),
        out_shape=jax.ShapeDtypeStruct((M, D), F32),
        compiler_params=_cparams(("parallel",)), name="merge",
    )(*args)


def _block_diag_queries(q):
    B, H, Tq, d = q.shape
    eye = jnp.eye(H, dtype=q.dtype)
    return (q[:, :, :, None, :] * eye[None, :, None, :, None]).reshape(B, H * Tq, H * d)


def _new_token_rows(t):
    B, H, Tq, d = t.shape
    rows = jnp.transpose(t, (0, 2, 1, 3)).reshape(B, Tq, H * d)
    return jnp.pad(rows, ((0, 0), (0, LANES - Tq), (0, 0)))


def _page_rows(page_refs, dtype):
    return jnp.concatenate([_heads_to_lanes(r).astype(dtype) for r in page_refs], axis=0)


def _take_block_diag(acc, o_ref, n_heads, tq):
    for h in range(n_heads):
        o_ref[0, h] = acc[h * tq:(h + 1) * tq, h * HEAD_DIM:(h + 1) * HEAD_DIM]


def _sb_sample_kernel(pt_ref, q_ref, *rest, n_heads, tq, n_steps, n_pg):
    k_refs, v_refs = rest[:n_pg], rest[n_pg:2 * n_pg]
    kn_ref, vn_ref, u_ref, un_ref, o_ref, carry_ref, acc_ref = rest[2 * n_pg:]
    p = pl.program_id(1)
    q = q_ref[0]
    rows = n_heads * tq

    @pl.when(p == 0)
    def _():
        carry_ref[...] = jnp.zeros_like(carry_ref)
        acc_ref[...] = jnp.zeros_like(acc_ref)
        row = lax.broadcasted_iota(jnp.int32, (rows, LANES), 0)
        col = lax.broadcasted_iota(jnp.int32, (rows, LANES), 1)
        _sb_tile([q], [kn_ref[0].astype(BF16)], [vn_ref[0].astype(BF16)], un_ref[...],
                 carry_ref, acc_ref, col < row % tq)

    _sb_tile([q], [_page_rows(k_refs, BF16)], [_page_rows(v_refs, BF16)], u_ref[...],
             carry_ref, acc_ref, None)

    @pl.when(p == n_steps - 1)
    def _():
        _take_block_diag(acc_ref[...], o_ref, n_heads, tq)


def _sb_sample(q_bd_bf, k_pool, v_pool, k_new_rows, v_new_rows, page_table, tq):
    B, rows, W = q_bd_bf.shape
    _, H, page, _ = k_pool.shape
    n_pg = SAMPLE_PAGES_PER_STEP
    n_steps = page_table.shape[1] // n_pg
    assert n_steps * n_pg == page_table.shape[1]

    def pg(j):
        return pl.BlockSpec((None, H, page, HEAD_DIM),
                            lambda b, p, pt: (pt[b, n_pg * (n_steps - 1 - p) + j], 0, 0, 0))

    per_b = lambda a: pl.BlockSpec((1,) + a.shape[1:], lambda b, p, pt: (b, 0, 0))
    u2, un2 = _suffix_matrix(2 * page), _suffix_matrix(LANES)
    const = lambda a: pl.BlockSpec(a.shape, lambda b, p, pt: (0, 0))
    pages = [pg(j) for j in range(n_pg)]
    grid_spec = pltpu.PrefetchScalarGridSpec(
        num_scalar_prefetch=1, grid=(B, n_steps),
        in_specs=[per_b(q_bd_bf), *pages, *pages, per_b(k_new_rows), per_b(v_new_rows),
                  const(u2), const(un2)],
        out_specs=pl.BlockSpec((1, H, tq, HEAD_DIM), lambda b, p, pt: (b, 0, 0, 0)),
        scratch_shapes=[pltpu.VMEM((rows, LANES), F32), pltpu.VMEM((rows, W), F32)])
    return pl.pallas_call(
        functools.partial(_sb_sample_kernel, n_heads=H, tq=tq, n_steps=n_steps, n_pg=n_pg),
        grid_spec=grid_spec, out_shape=jax.ShapeDtypeStruct((B, H, tq, HEAD_DIM), F32),
        compiler_params=_cparams(("parallel", "arbitrary")), name="sb_sample",
    )(page_table, q_bd_bf, *([k_pool] * n_pg), *([v_pool] * n_pg), k_new_rows, v_new_rows,
      u2, un2)


def _moba_sample_kernel(pt_ref, qf_ref, *rest, n_heads, tq, n_steps, n_pg):
    k_refs, v_refs = rest[:n_pg], rest[n_pg:2 * n_pg]
    kn_ref, vn_ref, o_ref, km_ref, m_ref, l_ref, ob_ref = rest[2 * n_pg:]
    n = pl.program_id(1)
    rows = n_heads * tq
    blk = MOBA_BLOCK
    bps = n_pg // 2
    nb = n_steps * bps
    qf = qf_ref[0]
    q = qf.astype(BF16)
    lane = lax.broadcasted_iota(jnp.int32, (rows, LANES), 1)
    diag = lambda o: jnp.concatenate(
        [o[h * tq:(h + 1) * tq, h * HEAD_DIM:(h + 1) * HEAD_DIM] for h in range(n_heads)], axis=0)

    @pl.when(n == 0)
    def _():
        km_ref[...] = jnp.zeros_like(km_ref)
        m_ref[...] = jnp.zeros_like(m_ref)
        l_ref[...] = jnp.zeros_like(l_ref)

    pages = [_heads_to_lanes(r) for r in k_refs]
    kc = jnp.concatenate([pg.astype(BF16) for pg in pages], axis=0)
    s = _dot_nt(q, kc) * ATTN_SCALE
    m_all, l_all = m_ref[...], l_ref[...]
    for c in range(bps):
        j = n * bps + c
        km_ref[pl.ds(j, 1), :] = (jnp.sum(pages[2 * c], axis=0, keepdims=True)
                                  + jnp.sum(pages[2 * c + 1], axis=0, keepdims=True)) * (1.0 / blk)
        sc = s[:, c * blk:(c + 1) * blk]
        mc = jnp.max(sc, axis=1, keepdims=True)
        pc = jnp.exp(sc - mc)
        ob_ref[j] = diag(jnp.dot(pc.astype(BF16), _page_rows(v_refs[2 * c:2 * c + 2], BF16),
                                 preferred_element_type=F32))
        m_all = jnp.where(lane == j, mc, m_all)
        l_all = jnp.where(lane == j, jnp.sum(pc, axis=1, keepdims=True), l_all)
    m_ref[...] = m_all
    l_ref[...] = l_all

    @pl.when(n == n_steps - 1)
    def _():
        gate = _dot_nt(qf, km_ref[...], precision=lax.Precision.HIGHEST)
        sel = _select_topk(gate, lane < nb) > 0.0
        row = lax.broadcasted_iota(jnp.int32, (rows, LANES), 0)
        s_new = jnp.where(lane <= row % tq, _dot_nt(q, kn_ref[0].astype(BF16)) * ATTN_SCALE,
                          NEG_BIG)
        m_blk = m_ref[...]
        m = jnp.maximum(jnp.max(s_new, axis=1, keepdims=True),
                        jnp.max(jnp.where(sel, m_blk, NEG_BIG), axis=1, keepdims=True))
        p_new = jnp.exp(s_new - m)
        w = jnp.where(sel, jnp.exp(jnp.where(sel, m_blk, NEG_BIG) - m), 0.0)
        l = (jnp.sum(p_new, axis=1, keepdims=True)
             + jnp.sum(w * l_ref[...], axis=1, keepdims=True))
        acc = diag(jnp.dot(p_new.astype(BF16), vn_ref[0].astype(BF16),
                           preferred_element_type=F32))
        for j in range(nb):
            acc = acc + w[:, j:j + 1] * ob_ref[j]
        o = acc / l
        for h in range(n_heads):
            o_ref[0, h] = o[h * tq:(h + 1) * tq]


def _moba_sample(q_bd_f32, k_pool, v_pool, k_new_rows, v_new_rows, page_table, tq):
    B, rows, W = q_bd_f32.shape
    _, H, page, _ = k_pool.shape
    assert 2 * page == MOBA_BLOCK
    n_pg = SAMPLE_PAGES_PER_STEP
    n_steps = page_table.shape[1] // n_pg
    assert n_steps * n_pg == page_table.shape[1] and n_pg % 2 == 0
    nb = page_table.shape[1] // 2
    assert nb <= LANES

    def pg(j):
        return pl.BlockSpec((None, H, page, HEAD_DIM),
                            lambda b, n, pt: (pt[b, n_pg * n + j], 0, 0, 0))

    per_b = lambda a: pl.BlockSpec((1,) + a.shape[1:], lambda b, n, pt: (b, 0, 0))
    pages = [pg(j) for j in range(n_pg)]
    grid_spec = pltpu.PrefetchScalarGridSpec(
        num_scalar_prefetch=1, grid=(B, n_steps),
        in_specs=[per_b(q_bd_f32), *pages, *pages, per_b(k_new_rows), per_b(v_new_rows)],
        out_specs=pl.BlockSpec((1, H, tq, HEAD_DIM), lambda b, n, pt: (b, 0, 0, 0)),
        scratch_shapes=[pltpu.VMEM((LANES, W), F32),
                        pltpu.VMEM((rows, LANES), F32),
                        pltpu.VMEM((rows, LANES), F32),
                        pltpu.VMEM((nb, rows, HEAD_DIM), F32)])
    return pl.pallas_call(
        functools.partial(_moba_sample_kernel, n_heads=H, tq=tq, n_steps=n_steps, n_pg=n_pg),
        grid_spec=grid_spec, out_shape=jax.ShapeDtypeStruct((B, H, tq, HEAD_DIM), F32),
        compiler_params=_cparams(("parallel", "arbitrary")), name="moba_sample",
    )(page_table, q_bd_f32, *([k_pool] * n_pg), *([v_pool] * n_pg), k_new_rows, v_new_rows)


def _rope_tables(pos):
    inv = ROPE_THETA ** (-jnp.arange(0, HEAD_DIM, 2, dtype=F32) / HEAD_DIM)
    ang = pos.astype(F32)[:, None] * inv[None, :]
    cos, sin = jnp.cos(ang), jnp.sin(ang)
    return jnp.concatenate([cos, cos], -1), jnp.concatenate([-sin, sin], -1)


def _project_all(x2d, w_in_bf, w_sb, w_moba, w_mem, tm, seq, rope_tabs, prompt):
    assert w_sb == w_moba and (8 * w_sb) % (2 * w_mem) == 0
    grp = 4 * w_sb
    kv = (F32, BF16) if prompt else (F32,)
    seg = lambda k, width, rope=False, rows=(), heads=(), headt=(), km=False: (
        k * width, width, rope, rows, heads, headt, km)
    a = _project(x2d, w_in_bf, 0, grp,
                 (seg(0, w_sb, heads=(BF16,) if prompt else (F32,)), seg(1, w_sb, heads=kv),
                  seg(2, w_sb, heads=kv), seg(3, w_sb, rows=(F32,))), tm, seq)
    b = _project(x2d, w_in_bf, 1, grp,
                 (seg(0, w_moba, rope=True, heads=(F32,)),
                  seg(1, w_moba, rope=True, heads=kv, km=prompt),
                  seg(2, w_moba, heads=(F32,), headt=(BF16,) if prompt else ()),
                  seg(3, w_moba, rows=(F32,))), tm, seq, rope_tabs=rope_tabs)
    qm, gm = _project(x2d, w_in_bf, (2 * grp) // (2 * w_mem), 2 * w_mem,
                      (seg(0, w_mem, heads=(F32,)), seg(1, w_mem, rows=(F32,))), tm, seq)
    if prompt:
        names_a = ("qa", "ka", "ka_bf", "va", "va_bf", "ga")
        names_b = ("qb", "kb", "kb_bf", "kmean", "vb", "vbt_bf", "gb")
    else:
        names_a = ("qa", "ka", "va", "ga")
        names_b = ("qb", "kb", "vb", "gb")
    out = dict(zip(names_a, a))
    out.update(zip(names_b, b))
    out.update(qm=qm, gm=gm)
    return out


def kernel(x_prompt, x_sample, cache_sb_k, cache_sb_v, cache_moba_k, cache_moba_v,
           cache_mem_k, cache_mem_v, page_table, mem_prompt,
           w_in, w_mem_k, w_mem_v, norm_a, norm_b, norm_m, w_out, ln_g, ln_b):
    depth = w_in.shape[0]
    B, T, D = x_prompt.shape
    Bs, Ts, _ = x_sample.shape
    h_sb, h_moba, h_mem = cache_sb_k.shape[3], cache_moba_k.shape[3], cache_mem_k.shape[3]
    w_sb, w_moba, w_mem = h_sb * HEAD_DIM, h_moba * HEAD_DIM, h_mem * HEAD_DIM
    n_mem = mem_prompt.shape[1]
    page = cache_sb_k.shape[2]
    past_len = page_table.shape[1] * page
    assert past_len % MOBA_BLOCK == 0 and Ts <= LANES and T % MOBA_BLOCK == 0
    alpha = (2.0 * depth) ** 0.25

    tm = 512
    rope_p = _rope_tables(jnp.arange(T, dtype=jnp.int32))
    rope_s = tuple(jnp.tile(t, (Bs, 1)) for t in
                   _rope_tables(past_len + jnp.arange(Ts, dtype=jnp.int32)))

    y_p = x_prompt.reshape(B * T, D)
    y_s = x_sample.reshape(Bs * Ts, D)
    outs = [[] for _ in range(10)]
    for l in range(depth):
        w_in_bf = w_in[l].astype(BF16)
        w_out_bf = w_out[l].astype(BF16)
        w_memkv_bf = jnp.concatenate([w_mem_k[l], w_mem_v[l]], axis=1).astype(BF16)

        pp = _project_all(y_p, w_in_bf, w_sb, w_moba, w_mem, tm, T, rope_p, True)
        o_a = _sb_prompt(pp["qa"], pp["ka_bf"], pp["va_bf"])
        kmean = jnp.transpose(pp["kmean"], (0, 2, 1, 3, 4)).reshape(
            B, h_moba, T // MOBA_BLOCK, HEAD_DIM)
        o_b = _moba_prompt(pp["qb"], pp["kb_bf"], pp["vbt_bf"], kmean)
        mk, mv = _project(
            mem_prompt.reshape(B * n_mem, D), w_memkv_bf, 0, 2 * w_mem,
            ((0, w_mem, False, (F32,), (), (), False),
             (w_mem, w_mem, False, (F32,), (), (), False)),
            256, n_mem)
        o_m = _mem_attn(pp["qm"], mk.reshape(B, n_mem, w_mem), mv.reshape(B, n_mem, w_mem), 1024)
        y_p_new = _merge(y_p, o_a, o_b, o_m, pp["ga"], pp["gb"], pp["gm"],
                         norm_a[l], norm_b[l], norm_m[l], w_out_bf, ln_g[l], ln_b[l], alpha, 512)

        ps = _project_all(y_s, w_in_bf, w_sb, w_moba, w_mem, Bs * Ts, Ts, rope_s, False)
        pool = lambda c: jnp.transpose(c[l], (0, 2, 1, 3))
        so_a = _sb_sample(_block_diag_queries(ps["qa"]).astype(BF16), pool(cache_sb_k),
                          pool(cache_sb_v), _new_token_rows(ps["ka"]), _new_token_rows(ps["va"]),
                          page_table, Ts)
        so_b = _moba_sample(_block_diag_queries(ps["qb"]), pool(cache_moba_k),
                            pool(cache_moba_v), _new_token_rows(ps["kb"]),
                            _new_token_rows(ps["vb"]), page_table, Ts)
        so_m = _mem_attn(ps["qm"], cache_mem_k[l].reshape(Bs, n_mem, w_mem),
                         cache_mem_v[l].reshape(Bs, n_mem, w_mem), Ts)
        y_s_new = _merge(y_s, so_a, so_b, so_m, ps["ga"], ps["gb"], ps["gm"],
                         norm_a[l], norm_b[l], norm_m[l], w_out_bf, ln_g[l], ln_b[l], alpha,
                         Bs * Ts)

        tok_major = lambda a: jnp.transpose(a, (0, 2, 1, 3))
        new = [tok_major(pp["ka"]), tok_major(pp["va"]), tok_major(pp["kb"]), tok_major(pp["vb"]),
               mk.reshape(B, n_mem, h_mem, HEAD_DIM), mv.reshape(B, n_mem, h_mem, HEAD_DIM),
               tok_major(ps["ka"]), tok_major(ps["va"]), tok_major(ps["kb"]), tok_major(ps["vb"])]
        for lst, a in zip(outs, new):
            lst.append(a)
        y_p, y_s = y_p_new, y_s_new

    return (y_p.reshape(B, T, D), y_s.reshape(Bs, Ts, D), *[jnp.stack(o) for o in outs])
```

```python
import functools

import jax
import jax.numpy as jnp
import numpy as np
from jax import lax
from jax.experimental import pallas as pl
from jax.experimental.pallas import tpu as pltpu

F32 = jnp.float32
BF16 = jnp.bfloat16

HEAD_DIM = 128
MOBA_BLOCK = 256
MOBA_TOPK = 3
ROPE_THETA = 10000.0
LN_EPS = 1e-5
RMS_EPS = 1e-6
ATTN_SCALE = HEAD_DIM ** -0.5
NEG_BIG = -1e30
LANES = 128
VMEM_LIMIT = 56 * 1024 * 1024
SAMPLE_PAGES_PER_STEP = 16


def _cparams(sem):
    return pltpu.CompilerParams(dimension_semantics=sem, vmem_limit_bytes=VMEM_LIMIT)


def _dot_nt(a, b, precision=None):
    return lax.dot_general(a, b, (((1,), (1,)), ((), ())), precision=precision,
                           preferred_element_type=F32)


def _heads_to_lanes(ref, idx=()):
    n_heads = ref.shape[len(idx)]
    return jnp.concatenate([ref[idx + (h,)] for h in range(n_heads)], axis=1)


def _proj_kernel(*refs, segs, has_rope, tm, seq):
    x_ref, w_ref = refs[0], refs[1]
    pos = 2
    if has_rope:
        cos_ref, sin_ref = refs[2], refs[3]
        pos = 4
    outs = refs[pos:]
    x = x_ref[...].astype(BF16)
    rpb = min(tm, seq)
    o = 0
    for c0, width, rope, row_dtypes, head_dtypes, headt_dtypes, want_kmean, scale in segs:
        r = jnp.dot(x, w_ref[:, c0:c0 + width], preferred_element_type=F32)
        n_heads = width // HEAD_DIM
        if scale is not None:
            r = r * scale
        heads = [r[:, h * HEAD_DIM:(h + 1) * HEAD_DIM] for h in range(n_heads)]
        if rope:
            cos = cos_ref[...]
            sin = sin_ref[...]
            heads = [xh * cos + pltpu.roll(xh, HEAD_DIM // 2, axis=1) * sin for xh in heads]
            r = jnp.concatenate(heads, axis=1)
        for dt in row_dtypes:
            outs[o][...] = r.astype(dt)
            o += 1
        for dt in head_dtypes:
            for bb in range(tm // rpb):
                for h in range(n_heads):
                    outs[o][bb, h] = heads[h][bb * rpb:(bb + 1) * rpb].astype(dt)
            o += 1
        for dt in headt_dtypes:
            for h in range(n_heads):
                outs[o][0, h] = heads[h].T.astype(dt)
            o += 1
        if want_kmean:
            nblk = tm // MOBA_BLOCK
            for h in range(n_heads):
                outs[o][0, 0, h] = jnp.sum(heads[h].reshape(nblk, MOBA_BLOCK, HEAD_DIM),
                                           axis=1) * (1.0 / MOBA_BLOCK)
            o += 1


def _project(x2d, w_bf, col_block, col_width, segs, tm, seq, rope_tabs=None):
    M, D = x2d.shape
    nt = M // tm
    n_batch = M // seq
    tpb = max(seq // tm, 1)
    rpb = min(tm, seq)
    has_rope = rope_tabs is not None
    in_specs = [pl.BlockSpec((tm, D), lambda i: (i, 0)),
                pl.BlockSpec((D, col_width), lambda i: (0, col_block))]
    args = [x2d, w_bf]
    if has_rope:
        in_specs += [pl.BlockSpec((tm, HEAD_DIM), lambda i: (i % tpb, 0))] * 2
        args += list(rope_tabs)
    out_shapes, out_specs = [], []
    for c0, width, rope, row_dtypes, head_dtypes, headt_dtypes, want_kmean, scale in segs:
        n_heads = width // HEAD_DIM
        for dt in row_dtypes:
            out_shapes.append(jax.ShapeDtypeStruct((M, width), dt))
            out_specs.append(pl.BlockSpec((tm, width), lambda i: (i, 0)))
        for dt in head_dtypes:
            out_shapes.append(jax.ShapeDtypeStruct((n_batch, n_heads, seq, HEAD_DIM), dt))
            out_specs.append(pl.BlockSpec((tm // rpb, n_heads, rpb, HEAD_DIM),
                                          lambda i: (i // tpb, 0, i % tpb, 0)))
        for dt in headt_dtypes:
            assert tm <= seq
            out_shapes.append(jax.ShapeDtypeStruct((n_batch, n_heads, HEAD_DIM, seq), dt))
            out_specs.append(pl.BlockSpec((1, n_heads, HEAD_DIM, tm),
                                          lambda i: (i // tpb, 0, 0, i % tpb)))
        if want_kmean:
            nblk = tm // MOBA_BLOCK
            out_shapes.append(jax.ShapeDtypeStruct((n_batch, tpb, n_heads, nblk, HEAD_DIM), F32))
            out_specs.append(pl.BlockSpec((1, 1, n_heads, nblk, HEAD_DIM),
                                          lambda i: (i // tpb, i % tpb, 0, 0, 0)))
    return pl.pallas_call(
        functools.partial(_proj_kernel, segs=segs, has_rope=has_rope, tm=tm, seq=seq),
        grid=(nt,), in_specs=in_specs, out_specs=out_specs, out_shape=out_shapes,
        compiler_params=_cparams(("parallel",)), name="proj",
    )(*args)


def _sb_tile(qs, kts, vts, u2, carry_ref, acc_ref, mask):
    ck = u2.shape[1]
    nch = kts[0].shape[0] // ck
    rows_h = qs[0].shape[0]
    rows = rows_h * len(qs)
    z = jnp.concatenate([_dot_nt(q, kt) for q, kt in zip(qs, kts)], axis=0)
    sp = jnp.log(1.0 + jnp.exp(-jnp.abs(z)))
    lb = jnp.minimum(z, 0.0) - sp
    l1m = lb - z
    if mask is not None:
        l1m = jnp.where(mask, l1m, 0.0)
    chunks = [l1m[:, c * ck:(c + 1) * ck] for c in range(nch)]
    st = jnp.concatenate(chunks, axis=0) if nch > 1 else l1m
    rest = jnp.dot(st.astype(BF16), u2, preferred_element_type=F32)
    carry = carry_ref[...]
    reps = ck // LANES
    a_parts = [None] * nch
    for c in reversed(range(nch)):
        rest_c = rest[c * rows:(c + 1) * rows]
        a_parts[c] = jnp.exp(lb[:, c * ck:(c + 1) * ck] + rest_c + jnp.tile(carry, (1, reps)))
        carry = carry + (rest_c[:, :1] + chunks[c][:, :1])
    a = jnp.concatenate(a_parts, axis=1) if nch > 1 else a_parts[0]
    if mask is not None:
        a = jnp.where(mask, a, 0.0)
    a = a.astype(BF16)
    for g, vt in enumerate(vts):
        sl = slice(g * rows_h, (g + 1) * rows_h)
        acc_ref[sl, :] += jnp.dot(a[sl], vt, preferred_element_type=F32)
    carry_ref[...] = carry


def _sb_prompt_kernel(q_ref, k_ref, v_ref, u_ref, o_ref, carry_ref, acc_ref, *, tq, n_grp):
    i = pl.program_id(2)
    u2 = u_ref[...]
    carry_ref[...] = jnp.zeros_like(carry_ref)
    acc_ref[...] = jnp.zeros_like(acc_ref)
    row = lax.broadcasted_iota(jnp.int32, (n_grp * tq, tq), 0)
    col = lax.broadcasted_iota(jnp.int32, (n_grp * tq, tq), 1)
    qs = [q_ref[0, g] for g in range(n_grp)]

    def tile(k0, mask):
        _sb_tile(qs, [k_ref[0, g, pl.ds(k0, tq), :] for g in range(n_grp)],
                 [v_ref[0, g, pl.ds(k0, tq), :] for g in range(n_grp)],
                 u2, carry_ref, acc_ref, mask)

    tile(pl.multiple_of(i * tq, tq), col < row % tq)

    def body(s, c):
        tile(pl.multiple_of((i - 1 - s) * tq, tq), None)
        return c

    lax.fori_loop(0, i, body, 0)
    for g in range(n_grp):
        o_ref[0, g] = acc_ref[g * tq:(g + 1) * tq, :]


def _suffix_matrix(n):
    u = (np.arange(n)[:, None] > np.arange(n)[None, :]).astype(np.float32)
    return jnp.asarray(u, dtype=BF16)


def _sb_prompt(q_bf, k_bf, v_bf, tq=256, n_grp=6):
    B, H, T, _ = q_bf.shape
    assert H % n_grp == 0
    q_spec = pl.BlockSpec((1, n_grp, tq, HEAD_DIM), lambda b, h, i: (b, h, i, 0))
    kv_spec = pl.BlockSpec((1, n_grp, T, HEAD_DIM), lambda b, h, i: (b, h, 0, 0))
    return pl.pallas_call(
        functools.partial(_sb_prompt_kernel, tq=tq, n_grp=n_grp),
        grid=(B, H // n_grp, T // tq),
        in_specs=[q_spec, kv_spec, kv_spec, pl.BlockSpec((tq, tq), lambda b, h, i: (0, 0))],
        out_specs=q_spec,
        out_shape=jax.ShapeDtypeStruct((B, H, T, HEAD_DIM), F32),
        scratch_shapes=[pltpu.VMEM((n_grp * tq, LANES), F32),
                        pltpu.VMEM((n_grp * tq, HEAD_DIM), F32)],
        compiler_params=_cparams(("parallel", "parallel", "arbitrary")), name="sb_prompt",
    )(q_bf, k_bf, v_bf, _suffix_matrix(tq))


def _select_topk(gate, past):
    lane = lax.broadcasted_iota(jnp.int32, gate.shape, 1)
    g = jnp.where(past, gate, -jnp.inf)
    sel = jnp.zeros(gate.shape, F32)
    for _ in range(MOBA_TOPK):
        m = jnp.max(g, axis=1, keepdims=True)
        idx = jnp.min(jnp.where(g == m, lane, LANES), axis=1, keepdims=True)
        hit = lane == idx
        sel = jnp.where(hit & (m > -jnp.inf), 1.0, sel)
        g = jnp.where(hit, -jnp.inf, g)
    return sel


def _rank_select(gate, n_past):
    nb = gate.shape[0]
    blk = lax.broadcasted_iota(jnp.int32, gate.shape, 0)
    rank = jnp.zeros(gate.shape, F32)
    for m in range(nb):
        gm = gate[m:m + 1, :]
        beats = (gm > gate) | ((gm == gate) & (blk > m))
        rank = rank + jnp.where(beats & (n_past > m), 1.0, 0.0)
    return jnp.where((blk < n_past) & (rank < MOBA_TOPK), 1.0, 0.0)


def _moba_prompt_kernel(q_ref, k_ref, vt_ref, vt_own_ref, km_ref, o_ref, m_ref, l_ref, acc_ref,
                        *, tq, nb, n_heads):
    i = pl.program_id(1)
    hs = range(n_heads)
    qf = [q_ref[0, g] for g in hs]
    q = [(x * ATTN_SCALE).astype(BF16) for x in qf]
    gate = jnp.concatenate(
        [_dot_nt(km_ref[0, g], qf[g], precision=lax.Precision.HIGHEST) for g in hs], axis=1)
    sel = _rank_select(gate, i)
    hcols = lambda g: slice(g * tq, (g + 1) * tq)

    def scores(keys_of):
        return jnp.concatenate([_dot_nt(keys_of(g), q[g]) for g in hs], axis=1)

    def weighted_values(p, vt_of):
        p = p.astype(BF16)
        return jnp.concatenate(
            [jnp.dot(vt_of(g), p[:, hcols(g)], preferred_element_type=F32) for g in hs], axis=1)

    key = lax.broadcasted_iota(jnp.int32, (tq, n_heads * tq), 0)
    qry = lax.broadcasted_iota(jnp.int32, (tq, n_heads * tq), 1) % tq
    d0 = pl.multiple_of(i * tq, tq)
    s = jnp.where(key <= qry, scores(lambda g: k_ref[0, g, pl.ds(d0, tq), :]), NEG_BIG)
    m = jnp.max(s, axis=0, keepdims=True)
    p = jnp.exp(s - m)
    m_ref[...] = m
    l_ref[...] = jnp.sum(p, axis=0, keepdims=True)
    acc_ref[...] = weighted_values(p, lambda g: vt_own_ref[0, g])

    for n in range(nb - 1):
        @pl.when(n < i)
        def _():
            keep = sel[n:n + 1, :] > 0.0
            s = jnp.where(keep, scores(lambda g: k_ref[0, g, n * tq:(n + 1) * tq, :]), NEG_BIG)
            m_old = m_ref[...]
            m_new = jnp.maximum(m_old, jnp.max(s, axis=0, keepdims=True))
            alpha = jnp.exp(m_old - m_new)
            p = jnp.exp(s - m_new)
            m_ref[...] = m_new
            l_ref[...] = alpha * l_ref[...] + jnp.sum(p, axis=0, keepdims=True)
            acc_ref[...] = alpha * acc_ref[...] + weighted_values(
                p, lambda g: vt_ref[0, g, :, n * tq:(n + 1) * tq])

    o = acc_ref[...] / l_ref[...]
    for g in hs:
        o_ref[0, g] = o[:, hcols(g)].T


def _moba_prompt(q_f32, k_bf, vt_bf, kmean):
    B, H, T, _ = q_f32.shape
    tq = MOBA_BLOCK
    nb = T // tq
    q_spec = pl.BlockSpec((1, H, tq, HEAD_DIM), lambda b, i: (b, 0, i, 0))
    return pl.pallas_call(
        functools.partial(_moba_prompt_kernel, tq=tq, nb=nb, n_heads=H),
        grid=(B, nb),
        in_specs=[q_spec,
                  pl.BlockSpec((1, H, T, HEAD_DIM), lambda b, i: (b, 0, 0, 0)),
                  pl.BlockSpec((1, H, HEAD_DIM, T), lambda b, i: (b, 0, 0, 0)),
                  pl.BlockSpec((1, H, HEAD_DIM, tq), lambda b, i: (b, 0, 0, i)),
                  pl.BlockSpec((1, H, nb, HEAD_DIM), lambda b, i: (b, 0, 0, 0))],
        out_specs=q_spec,
        out_shape=jax.ShapeDtypeStruct((B, H, T, HEAD_DIM), F32),
        scratch_shapes=[pltpu.VMEM((1, H * tq), F32), pltpu.VMEM((1, H * tq), F32),
                        pltpu.VMEM((HEAD_DIM, H * tq), F32)],
        compiler_params=_cparams(("parallel", "arbitrary")), name="moba_prompt",
    )(q_f32, k_bf, vt_bf, vt_bf, kmean)


def _mem_attn_kernel(q_ref, k_ref, v_ref, o_ref, *, n_heads):
    for h in range(n_heads):
        cols = slice(h * HEAD_DIM, (h + 1) * HEAD_DIM)
        s = _dot_nt((q_ref[0, h] * ATTN_SCALE).astype(BF16), k_ref[0, :, cols].astype(BF16))
        m = jnp.max(s, axis=1, keepdims=True)
        p = jnp.exp(s - m)
        l = jnp.sum(p, axis=1, keepdims=True)
        o = jnp.dot(p.astype(BF16), v_ref[0, :, cols].astype(BF16), preferred_element_type=F32)
        o_ref[0, h] = o / l


def _mem_attn(q, mk, mv, tq):
    B, H, T, _ = q.shape
    n_mem = mk.shape[1]
    q_spec = pl.BlockSpec((1, H, tq, HEAD_DIM), lambda b, i: (b, 0, i, 0))
    kv_spec = pl.BlockSpec((1, n_mem, H * HEAD_DIM), lambda b, i: (b, 0, 0))
    return pl.pallas_call(
        functools.partial(_mem_attn_kernel, n_heads=H),
        grid=(B, T // tq),
        in_specs=[q_spec, kv_spec, kv_spec],
        out_specs=q_spec,
        out_shape=jax.ShapeDtypeStruct((B, H, T, HEAD_DIM), F32),
        compiler_params=_cparams(("parallel", "arbitrary")), name="mem_attn",
    )(q, mk, mv)


def _merge_kernel(x_ref, oa_ref, ob_ref, om_ref, ga_ref, gb_ref, gm_ref,
                  na_ref, nb_ref, nm_ref, w_ref, lg_ref, lb_ref, y_ref, *, alpha):
    def group(o_ref, g_ref, n_ref):
        o = jnp.concatenate([_heads_to_lanes(o_ref, (bb,)) for bb in range(o_ref.shape[0])],
                            axis=0)
        r = o * lax.rsqrt(jnp.mean(o * o, axis=-1, keepdims=True) + RMS_EPS) * n_ref[...]
        g = g_ref[...]
        return (r * (g * (1.0 / (1.0 + jnp.exp(-g))))).astype(BF16)

    mix = jnp.concatenate([group(oa_ref, ga_ref, na_ref), group(ob_ref, gb_ref, nb_ref),
                           group(om_ref, gm_ref, nm_ref)], axis=1)
    sub = jnp.dot(mix, w_ref[...], preferred_element_type=F32)
    h = alpha * x_ref[...] + sub
    mu = jnp.mean(h, axis=-1, keepdims=True)
    d = h - mu
    var = jnp.mean(d * d, axis=-1, keepdims=True)
    y_ref[...] = d * lax.rsqrt(var + LN_EPS) * lg_ref[...] + lb_ref[...]


def _merge(x2d, o_a, o_b, o_m, g_a, g_b, g_m, norm_a, norm_b, norm_m, w_out_bf, ln_g, ln_b,
           alpha, tm):
    M, D = x2d.shape
    seq = o_a.shape[2]
    tpb = max(seq // tm, 1)
    rpb = min(tm, seq)
    row = lambda a: pl.BlockSpec((tm, a.shape[1]), lambda i: (i, 0))
    head = lambda a: pl.BlockSpec((tm // rpb, a.shape[1], rpb, HEAD_DIM),
                                  lambda i: (i // tpb, 0, i % tpb, 0))
    full = lambda a: pl.BlockSpec(a.shape, lambda i: (0, 0))
    vecs = [v.reshape(1, -1) for v in (norm_a, norm_b, norm_m, ln_g, ln_b)]
    args = [x2d, o_a, o_b, o_m, g_a, g_b, g_m, *vecs[:3], w_out_bf, *vecs[3:]]
    in_specs = ([row(x2d)] + [head(a) for a in (o_a, o_b, o_m)] + [row(a) for a in (g_a, g_b, g_m)]
                + [full(a) for a in args[7:]])
    return pl.pallas_call(
        functools.partial(_merge_kernel, alpha=alpha),
        grid=(M // tm,), in_specs=in_specs,
        out_specs=pl.BlockSpec((tm, D), lambda i: (i, 0)),
        out_shape=jax.ShapeDtypeStruct((M, D), F32),
        compiler_params=_cparams(("parallel",)), name="merge",
    )(*args)


def _block_diag_queries(q):
    B, H, Tq, d = q.shape
    eye = jnp.eye(H, dtype=q.dtype)
    return (q[:, :, :, None, :] * eye[None, :, None, :, None]).reshape(B, H * Tq, H * d)


def _new_token_rows(t):
    B, H, Tq, d = t.shape
    rows = jnp.transpose(t, (0, 2, 1, 3)).reshape(B, Tq, H * d)
    return jnp.pad(rows, ((0, 0), (0, LANES - Tq), (0, 0)))


def _page_rows(page_refs, dtype):
    return jnp.concatenate([_heads_to_lanes(r).astype(dtype) for r in page_refs], axis=0)


def _take_block_diag(acc, o_ref, n_heads, tq):
    for h in range(n_heads):
        o_ref[0, h] = acc[h * tq:(h + 1) * tq, h * HEAD_DIM:(h + 1) * HEAD_DIM]


def _sb_sample_kernel(pt_ref, q_ref, *rest, n_heads, tq, n_steps, n_pg):
    k_refs, v_refs = rest[:n_pg], rest[n_pg:2 * n_pg]
    kn_ref, vn_ref, u_ref, un_ref, o_ref, carry_ref, acc_ref = rest[2 * n_pg:]
    p = pl.program_id(1)
    q = q_ref[0]
    rows = n_heads * tq

    @pl.when(p == 0)
    def _():
        carry_ref[...] = jnp.zeros_like(carry_ref)
        acc_ref[...] = jnp.zeros_like(acc_ref)
        row = lax.broadcasted_iota(jnp.int32, (rows, LANES), 0)
        col = lax.broadcasted_iota(jnp.int32, (rows, LANES), 1)
        _sb_tile([q], [kn_ref[0].astype(BF16)], [vn_ref[0].astype(BF16)], un_ref[...],
                 carry_ref, acc_ref, col < row % tq)

    _sb_tile([q], [_page_rows(k_refs, BF16)], [_page_rows(v_refs, BF16)], u_ref[...],
             carry_ref, acc_ref, None)

    @pl.when(p == n_steps - 1)
    def _():
        _take_block_diag(acc_ref[...], o_ref, n_heads, tq)


def _sb_sample(q_bd_bf, k_pool, v_pool, k_new_rows, v_new_rows, page_table, tq):
    B, rows, W = q_bd_bf.shape
    _, H, page, _ = k_pool.shape
    n_pg = SAMPLE_PAGES_PER_STEP
    n_steps = page_table.shape[1] // n_pg
    assert n_steps * n_pg == page_table.shape[1]

    def pg(j):
        return pl.BlockSpec((None, H, page, HEAD_DIM),
                            lambda b, p, pt: (pt[b, n_pg * (n_steps - 1 - p) + j], 0, 0, 0))

    per_b = lambda a: pl.BlockSpec((1,) + a.shape[1:], lambda b, p, pt: (b, 0, 0))
    u2, un2 = _suffix_matrix(2 * page), _suffix_matrix(LANES)
    const = lambda a: pl.BlockSpec(a.shape, lambda b, p, pt: (0, 0))
    pages = [pg(j) for j in range(n_pg)]
    grid_spec = pltpu.PrefetchScalarGridSpec(
        num_scalar_prefetch=1, grid=(B, n_steps),
        in_specs=[per_b(q_bd_bf), *pages, *pages, per_b(k_new_rows), per_b(v_new_rows),
                  const(u2), const(un2)],
        out_specs=pl.BlockSpec((1, H, tq, HEAD_DIM), lambda b, p, pt: (b, 0, 0, 0)),
        scratch_shapes=[pltpu.VMEM((rows, LANES), F32), pltpu.VMEM((rows, W), F32)])
    return pl.pallas_call(
        functools.partial(_sb_sample_kernel, n_heads=H, tq=tq, n_steps=n_steps, n_pg=n_pg),
        grid_spec=grid_spec, out_shape=jax.ShapeDtypeStruct((B, H, tq, HEAD_DIM), F32),
        compiler_params=_cparams(("parallel", "arbitrary")), name="sb_sample",
    )(page_table, q_bd_bf, *([k_pool] * n_pg), *([v_pool] * n_pg), k_new_rows, v_new_rows,
      u2, un2)


def _moba_sample_kernel(pt_ref, qf_ref, *rest, n_heads, tq, n_steps, n_pg):
    k_refs, v_refs = rest[:n_pg], rest[n_pg:2 * n_pg]
    kn_ref, vn_ref, o_ref, km_ref, m_ref, l_ref, ob_ref = rest[2 * n_pg:]
    n = pl.program_id(1)
    rows = n_heads * tq
    blk = MOBA_BLOCK
    bps = n_pg // 2
    nb = n_steps * bps
    qf = qf_ref[0]
    q = (qf * ATTN_SCALE).astype(BF16)
    lane = lax.broadcasted_iota(jnp.int32, (rows, LANES), 1)
    diag = lambda o: jnp.concatenate(
        [o[h * tq:(h + 1) * tq, h * HEAD_DIM:(h + 1) * HEAD_DIM] for h in range(n_heads)], axis=0)

    @pl.when(n == 0)
    def _():
        km_ref[...] = jnp.zeros_like(km_ref)
        m_ref[...] = jnp.zeros_like(m_ref)
        l_ref[...] = jnp.zeros_like(l_ref)

    pages = [_heads_to_lanes(r) for r in k_refs]
    kc = jnp.concatenate([pg.astype(BF16) for pg in pages], axis=0)
    s = _dot_nt(q, kc)
    m_all, l_all = m_ref[...], l_ref[...]
    for c in range(bps):
        j = n * bps + c
        km_ref[pl.ds(j, 1), :] = (jnp.sum(pages[2 * c], axis=0, keepdims=True)
                                  + jnp.sum(pages[2 * c + 1], axis=0, keepdims=True)) * (1.0 / blk)
        sc = s[:, c * blk:(c + 1) * blk]
        mc = jnp.max(sc, axis=1, keepdims=True)
        pc = jnp.exp(sc - mc)
        ob_ref[j] = diag(jnp.dot(pc.astype(BF16), _page_rows(v_refs[2 * c:2 * c + 2], BF16),
                                 preferred_element_type=F32))
        m_all = jnp.where(lane == j, mc, m_all)
        l_all = jnp.where(lane == j, jnp.sum(pc, axis=1, keepdims=True), l_all)
    m_ref[...] = m_all
    l_ref[...] = l_all

    @pl.when(n == n_steps - 1)
    def _():
        gate = _dot_nt(qf, km_ref[...], precision=lax.Precision.HIGHEST)
        sel = _select_topk(gate, lane < nb) > 0.0
        row = lax.broadcasted_iota(jnp.int32, (rows, LANES), 0)
        s_new = jnp.where(lane <= row % tq, _dot_nt(q, kn_ref[0].astype(BF16)), NEG_BIG)
        m_blk = m_ref[...]
        m = jnp.maximum(jnp.max(s_new, axis=1, keepdims=True),
                        jnp.max(jnp.where(sel, m_blk, NEG_BIG), axis=1, keepdims=True))
        p_new = jnp.exp(s_new - m)
        w = jnp.where(sel, jnp.exp(jnp.where(sel, m_blk, NEG_BIG) - m), 0.0)
        l = (jnp.sum(p_new, axis=1, keepdims=True)
             + jnp.sum(w * l_ref[...], axis=1, keepdims=True))
        acc = diag(jnp.dot(p_new.astype(BF16), vn_ref[0].astype(BF16),
                           preferred_element_type=F32))
        for j in range(nb):
            acc = acc + w[:, j:j + 1] * ob_ref[j]
        o = acc / l
        for h in range(n_heads):
            o_ref[0, h] = o[h * tq:(h + 1) * tq]


def _moba_sample(q_bd_f32, k_pool, v_pool, k_new_rows, v_new_rows, page_table, tq):
    B, rows, W = q_bd_f32.shape
    _, H, page, _ = k_pool.shape
    assert 2 * page == MOBA_BLOCK
    n_pg = SAMPLE_PAGES_PER_STEP
    n_steps = page_table.shape[1] // n_pg
    assert n_steps * n_pg == page_table.shape[1] and n_pg % 2 == 0
    nb = page_table.shape[1] // 2
    assert nb <= LANES

    def pg(j):
        return pl.BlockSpec((None, H, page, HEAD_DIM),
                            lambda b, n, pt: (pt[b, n_pg * n + j], 0, 0, 0))

    per_b = lambda a: pl.BlockSpec((1,) + a.shape[1:], lambda b, n, pt: (b, 0, 0))
    pages = [pg(j) for j in range(n_pg)]
    grid_spec = pltpu.PrefetchScalarGridSpec(
        num_scalar_prefetch=1, grid=(B, n_steps),
        in_specs=[per_b(q_bd_f32), *pages, *pages, per_b(k_new_rows), per_b(v_new_rows)],
        out_specs=pl.BlockSpec((1, H, tq, HEAD_DIM), lambda b, n, pt: (b, 0, 0, 0)),
        scratch_shapes=[pltpu.VMEM((LANES, W), F32),
                        pltpu.VMEM((rows, LANES), F32),
                        pltpu.VMEM((rows, LANES), F32),
                        pltpu.VMEM((nb, rows, HEAD_DIM), F32)])
    return pl.pallas_call(
        functools.partial(_moba_sample_kernel, n_heads=H, tq=tq, n_steps=n_steps, n_pg=n_pg),
        grid_spec=grid_spec, out_shape=jax.ShapeDtypeStruct((B, H, tq, HEAD_DIM), F32),
        compiler_params=_cparams(("parallel", "arbitrary")), name="moba_sample",
    )(page_table, q_bd_f32, *([k_pool] * n_pg), *([v_pool] * n_pg), k_new_rows, v_new_rows)


def _rope_tables(pos):
    inv = ROPE_THETA ** (-jnp.arange(0, HEAD_DIM, 2, dtype=F32) / HEAD_DIM)
    ang = pos.astype(F32)[:, None] * inv[None, :]
    cos, sin = jnp.cos(ang), jnp.sin(ang)
    return jnp.concatenate([cos, cos], -1), jnp.concatenate([-sin, sin], -1)


def _project_all(x2d, w_in_bf, w_sb, w_moba, w_mem, tm, seq, rope_tabs, prompt):
    assert w_sb == w_moba and (8 * w_sb) % (2 * w_mem) == 0
    grp = 4 * w_sb
    kv = (F32, BF16) if prompt else (F32,)
    seg = lambda k, width, rope=False, rows=(), heads=(), headt=(), km=False, scale=None: (
        k * width, width, rope, rows, heads, headt, km, scale)
    a = _project(x2d, w_in_bf, 0, grp,
                 (seg(0, w_sb, heads=(BF16,) if prompt else (F32,), scale=ATTN_SCALE),
                  seg(1, w_sb, heads=kv),
                  seg(2, w_sb, heads=kv), seg(3, w_sb, rows=(F32,))), tm, seq)
    b = _project(x2d, w_in_bf, 1, grp,
                 (seg(0, w_moba, rope=True, heads=(F32,)),
                  seg(1, w_moba, rope=True, heads=kv, km=prompt),
                  seg(2, w_moba, heads=(F32,), headt=(BF16,) if prompt else ()),
                  seg(3, w_moba, rows=(F32,))), tm, seq, rope_tabs=rope_tabs)
    qm, gm = _project(x2d, w_in_bf, (2 * grp) // (2 * w_mem), 2 * w_mem,
                      (seg(0, w_mem, heads=(F32,)), seg(1, w_mem, rows=(F32,))), tm, seq)
    if prompt:
        names_a = ("qa", "ka", "ka_bf", "va", "va_bf", "ga")
        names_b = ("qb", "kb", "kb_bf", "kmean", "vb", "vbt_bf", "gb")
    else:
        names_a = ("qa", "ka", "va", "ga")
        names_b = ("qb", "kb", "vb", "gb")
    out = dict(zip(names_a, a))
    out.update(zip(names_b, b))
    out.update(qm=qm, gm=gm)
    return out


def kernel(x_prompt, x_sample, cache_sb_k, cache_sb_v, cache_moba_k, cache_moba_v,
           cache_mem_k, cache_mem_v, page_table, mem_prompt,
           w_in, w_mem_k, w_mem_v, norm_a, norm_b, norm_m, w_out, ln_g, ln_b):
    depth = w_in.shape[0]
    B, T, D = x_prompt.shape
    Bs, Ts, _ = x_sample.shape
    h_sb, h_moba, h_mem = cache_sb_k.shape[3], cache_moba_k.shape[3], cache_mem_k.shape[3]
    w_sb, w_moba, w_mem = h_sb * HEAD_DIM, h_moba * HEAD_DIM, h_mem * HEAD_DIM
    n_mem = mem_prompt.shape[1]
    page = cache_sb_k.shape[2]
    past_len = page_table.shape[1] * page
    assert past_len % MOBA_BLOCK == 0 and Ts <= LANES and T % MOBA_BLOCK == 0
    alpha = (2.0 * depth) ** 0.25

    tm = 512
    rope_p = _rope_tables(jnp.arange(T, dtype=jnp.int32))
    rope_s = tuple(jnp.tile(t, (Bs, 1)) for t in
                   _rope_tables(past_len + jnp.arange(Ts, dtype=jnp.int32)))

    y_p = x_prompt.reshape(B * T, D)
    y_s = x_sample.reshape(Bs * Ts, D)
    outs = [[] for _ in range(10)]
    for l in range(depth):
        w_in_bf = w_in[l].astype(BF16)
        w_out_bf = w_out[l].astype(BF16)
        w_memkv_bf = jnp.concatenate([w_mem_k[l], w_mem_v[l]], axis=1).astype(BF16)

        pp = _project_all(y_p, w_in_bf, w_sb, w_moba, w_mem, tm, T, rope_p, True)
        o_a = _sb_prompt(pp["qa"], pp["ka_bf"], pp["va_bf"])
        kmean = jnp.transpose(pp["kmean"], (0, 2, 1, 3, 4)).reshape(
            B, h_moba, T // MOBA_BLOCK, HEAD_DIM)
        o_b = _moba_prompt(pp["qb"], pp["kb_bf"], pp["vbt_bf"], kmean)
        mk, mv = _project(
            mem_prompt.reshape(B * n_mem, D), w_memkv_bf, 0, 2 * w_mem,
            ((0, w_mem, False, (F32,), (), (), False, None),
             (w_mem, w_mem, False, (F32,), (), (), False, None)),
            256, n_mem)
        o_m = _mem_attn(pp["qm"], mk.reshape(B, n_mem, w_mem), mv.reshape(B, n_mem, w_mem), 1024)
        y_p_new = _merge(y_p, o_a, o_b, o_m, pp["ga"], pp["gb"], pp["gm"],
                         norm_a[l], norm_b[l], norm_m[l], w_out_bf, ln_g[l], ln_b[l], alpha, 512)

        ps = _project_all(y_s, w_in_bf, w_sb, w_moba, w_mem, Bs * Ts, Ts, rope_s, False)
        pool = lambda c: jnp.transpose(c[l], (0, 2, 1, 3))
        so_a = _sb_sample(_block_diag_queries(ps["qa"]).astype(BF16), pool(cache_sb_k),
                          pool(cache_sb_v), _new_token_rows(ps["ka"]), _new_token_rows(ps["va"]),
                          page_table, Ts)
        so_b = _moba_sample(_block_diag_queries(ps["qb"]), pool(cache_moba_k),
                            pool(cache_moba_v), _new_token_rows(ps["kb"]),
                            _new_token_rows(ps["vb"]), page_table, Ts)
        so_m = _mem_attn(ps["qm"], cache_mem_k[l].reshape(Bs, n_mem, w_mem),
                         cache_mem_v[l].reshape(Bs, n_mem, w_mem), Ts)
        y_s_new = _merge(y_s, so_a, so_b, so_m, ps["ga"], ps["gb"], ps["gm"],
                         norm_a[l], norm_b[l], norm_m[l], w_out_bf, ln_g[l], ln_b[l], alpha,
                         Bs * Ts)

        tok_major = lambda a: jnp.transpose(a, (0, 2, 1, 3))
        new = [tok_major(pp["ka"]), tok_major(pp["va"]), tok_major(pp["kb"]), tok_major(pp["vb"]),
               mk.reshape(B, n_mem, h_mem, HEAD_DIM), mv.reshape(B, n_mem, h_mem, HEAD_DIM),
               tok_major(ps["ka"]), tok_major(ps["va"]), tok_major(ps["kb"]), tok_major(ps["vb"])]
        for lst, a in zip(outs, new):
            lst.append(a)
        y_p, y_s = y_p_new, y_s_new

    return (y_p.reshape(B, T, D), y_s.reshape(Bs, Ts, D), *[jnp.stack(o) for o in outs])
```

```python
import functools

import jax
import jax.numpy as jnp
import numpy as np
from jax import lax
from jax.experimental import pallas as pl
from jax.experimental.pallas import tpu as pltpu

F32 = jnp.float32
BF16 = jnp.bfloat16

HEAD_DIM = 128
MOBA_BLOCK = 256
MOBA_TOPK = 3
ROPE_THETA = 10000.0
LN_EPS = 1e-5
RMS_EPS = 1e-6
ATTN_SCALE = HEAD_DIM ** -0.5
NEG_BIG = -1e30
LANES = 128
VMEM_LIMIT = 56 * 1024 * 1024
SAMPLE_PAGES_PER_STEP = 16


def _cparams(sem):
    return pltpu.CompilerParams(dimension_semantics=sem, vmem_limit_bytes=VMEM_LIMIT)


def _dot_nt(a, b, precision=None):
    return lax.dot_general(a, b, (((1,), (1,)), ((), ())), precision=precision,
                           preferred_element_type=F32)


def _heads_to_lanes(ref, idx=()):
    n_heads = ref.shape[len(idx)]
    return jnp.concatenate([ref[idx + (h,)] for h in range(n_heads)], axis=1)


def _proj_kernel(*refs, segs, has_rope, tm, seq):
    x_ref, w_ref = refs[0], refs[1]
    pos = 2
    if has_rope:
        cos_ref, sin_ref = refs[2], refs[3]
        pos = 4
    outs = refs[pos:]
    x = x_ref[...].astype(BF16)
    rpb = min(tm, seq)
    o = 0
    for c0, width, rope, row_dtypes, head_dtypes, headt_dtypes, want_kmean, scale in segs:
        r = jnp.dot(x, w_ref[:, c0:c0 + width], preferred_element_type=F32)
        n_heads = width // HEAD_DIM
        if scale is not None:
            r = r * scale
        heads = [r[:, h * HEAD_DIM:(h + 1) * HEAD_DIM] for h in range(n_heads)]
        if rope:
            cos = cos_ref[...]
            sin = sin_ref[...]
            heads = [xh * cos + pltpu.roll(xh, HEAD_DIM // 2, axis=1) * sin for xh in heads]
            r = jnp.concatenate(heads, axis=1)
        for dt in row_dtypes:
            outs[o][...] = r.astype(dt)
            o += 1
        for dt in head_dtypes:
            for bb in range(tm // rpb):
                for h in range(n_heads):
                    outs[o][bb, h] = heads[h][bb * rpb:(bb + 1) * rpb].astype(dt)
            o += 1
        for dt in headt_dtypes:
            for h in range(n_heads):
                outs[o][0, h] = heads[h].T.astype(dt)
            o += 1
        if want_kmean:
            nblk = tm // MOBA_BLOCK
            for h in range(n_heads):
                outs[o][0, 0, h] = jnp.sum(heads[h].reshape(nblk, MOBA_BLOCK, HEAD_DIM),
                                           axis=1) * (1.0 / MOBA_BLOCK)
            o += 1


def _project(x2d, w_bf, col_block, col_width, segs, tm, seq, rope_tabs=None):
    M, D = x2d.shape
    nt = M // tm
    n_batch = M // seq
    tpb = max(seq // tm, 1)
    rpb = min(tm, seq)
    has_rope = rope_tabs is not None
    in_specs = [pl.BlockSpec((tm, D), lambda i: (i, 0)),
                pl.BlockSpec((D, col_width), lambda i: (0, col_block))]
    args = [x2d, w_bf]
    if has_rope:
        in_specs += [pl.BlockSpec((tm, HEAD_DIM), lambda i: (i % tpb, 0))] * 2
        args += list(rope_tabs)
    out_shapes, out_specs = [], []
    for c0, width, rope, row_dtypes, head_dtypes, headt_dtypes, want_kmean, scale in segs:
        n_heads = width // HEAD_DIM
        for dt in row_dtypes:
            out_shapes.append(jax.ShapeDtypeStruct((M, width), dt))
            out_specs.append(pl.BlockSpec((tm, width), lambda i: (i, 0)))
        for dt in head_dtypes:
            out_shapes.append(jax.ShapeDtypeStruct((n_batch, n_heads, seq, HEAD_DIM), dt))
            out_specs.append(pl.BlockSpec((tm // rpb, n_heads, rpb, HEAD_DIM),
                                          lambda i: (i // tpb, 0, i % tpb, 0)))
        for dt in headt_dtypes:
            assert tm <= seq
            out_shapes.append(jax.ShapeDtypeStruct((n_batch, n_heads, HEAD_DIM, seq), dt))
            out_specs.append(pl.BlockSpec((1, n_heads, HEAD_DIM, tm),
                                          lambda i: (i // tpb, 0, 0, i % tpb)))
        if want_kmean:
            nblk = tm // MOBA_BLOCK
            out_shapes.append(jax.ShapeDtypeStruct((n_batch, tpb, n_heads, nblk, HEAD_DIM), F32))
            out_specs.append(pl.BlockSpec((1, 1, n_heads, nblk, HEAD_DIM),
                                          lambda i: (i // tpb, i % tpb, 0, 0, 0)))
    return pl.pallas_call(
        functools.partial(_proj_kernel, segs=segs, has_rope=has_rope, tm=tm, seq=seq),
        grid=(nt,), in_specs=in_specs, out_specs=out_specs, out_shape=out_shapes,
        compiler_params=_cparams(("parallel",)), name="proj",
    )(*args)


def _sb_logits(qs, kts):
    return jnp.concatenate([_dot_nt(q, kt) for q, kt in zip(qs, kts)], axis=0)


def _sb_tile(z, vts, u2, carry_ref, acc_ref, mask):
    ck = u2.shape[1]
    nch = z.shape[1] // ck
    rows = z.shape[0]
    rows_h = rows // len(vts)
    sp = jnp.maximum(z, 0.0) + jnp.log(1.0 + jnp.exp(-jnp.abs(z)))
    if mask is not None:
        sp = jnp.where(mask, sp, 0.0)
    st = (jnp.concatenate([sp[:, c * ck:(c + 1) * ck] for c in range(nch)], axis=0)
          if nch > 1 else sp)
    tail = jnp.dot(st.astype(BF16), u2, preferred_element_type=F32)
    carry = carry_ref[...]
    reps = ck // LANES
    a_parts = [None] * nch
    for c in reversed(range(nch)):
        tail_c = tail[c * rows:(c + 1) * rows]
        a_parts[c] = jnp.exp(z[:, c * ck:(c + 1) * ck] - tail_c - jnp.tile(carry, (1, reps)))
        carry = carry + tail_c[:, :1]
    a = jnp.concatenate(a_parts, axis=1) if nch > 1 else a_parts[0]
    if mask is not None:
        a = jnp.where(mask, a, 0.0)
    a = a.astype(BF16)
    for g, vt in enumerate(vts):
        sl = slice(g * rows_h, (g + 1) * rows_h)
        acc_ref[sl, :] += jnp.dot(a[sl], vt, preferred_element_type=F32)
    carry_ref[...] = carry


def _sb_prompt_kernel(q_ref, k_ref, v_ref, u_ref, o_ref, carry_ref, acc_ref, *, tq, n_grp):
    i = pl.program_id(2)
    u2 = u_ref[...]
    carry_ref[...] = jnp.zeros_like(carry_ref)
    acc_ref[...] = jnp.zeros_like(acc_ref)
    row = lax.broadcasted_iota(jnp.int32, (n_grp * tq, tq), 0)
    col = lax.broadcasted_iota(jnp.int32, (n_grp * tq, tq), 1)
    qs = [q_ref[0, g] for g in range(n_grp)]

    def logits(tile_idx):
        k0 = pl.multiple_of(tile_idx * tq, tq)
        return _sb_logits(qs, [k_ref[0, g, pl.ds(k0, tq), :] for g in range(n_grp)])

    def values(tile_idx):
        k0 = pl.multiple_of(tile_idx * tq, tq)
        return [v_ref[0, g, pl.ds(k0, tq), :] for g in range(n_grp)]

    _sb_tile(logits(i), values(i), u2, carry_ref, acc_ref, col < row % tq)

    def body(s, c):
        j = i - 1 - s
        _sb_tile(logits(j), values(j), u2, carry_ref, acc_ref, None)
        return c

    lax.fori_loop(0, i, body, 0)
    for g in range(n_grp):
        o_ref[0, g] = acc_ref[g * tq:(g + 1) * tq, :]


def _suffix_matrix(n):
    u = (np.arange(n)[:, None] >= np.arange(n)[None, :]).astype(np.float32)
    return jnp.asarray(u, dtype=BF16)


def _sb_prompt(q_bf, k_bf, v_bf, tq=256, n_grp=6):
    B, H, T, _ = q_bf.shape
    assert H % n_grp == 0
    q_spec = pl.BlockSpec((1, n_grp, tq, HEAD_DIM), lambda b, h, i: (b, h, i, 0))
    kv_spec = pl.BlockSpec((1, n_grp, T, HEAD_DIM), lambda b, h, i: (b, h, 0, 0))
    return pl.pallas_call(
        functools.partial(_sb_prompt_kernel, tq=tq, n_grp=n_grp),
        grid=(B, H // n_grp, T // tq),
        in_specs=[q_spec, kv_spec, kv_spec, pl.BlockSpec((tq, tq), lambda b, h, i: (0, 0))],
        out_specs=q_spec,
        out_shape=jax.ShapeDtypeStruct((B, H, T, HEAD_DIM), F32),
        scratch_shapes=[pltpu.VMEM((n_grp * tq, LANES), F32),
                        pltpu.VMEM((n_grp * tq, HEAD_DIM), F32)],
        compiler_params=_cparams(("parallel", "parallel", "arbitrary")), name="sb_prompt",
    )(q_bf, k_bf, v_bf, _suffix_matrix(tq))


def _select_topk(gate, past):
    lane = lax.broadcasted_iota(jnp.int32, gate.shape, 1)
    g = jnp.where(past, gate, -jnp.inf)
    sel = jnp.zeros(gate.shape, F32)
    for _ in range(MOBA_TOPK):
        m = jnp.max(g, axis=1, keepdims=True)
        idx = jnp.min(jnp.where(g == m, lane, LANES), axis=1, keepdims=True)
        hit = lane == idx
        sel = jnp.where(hit & (m > -jnp.inf), 1.0, sel)
        g = jnp.where(hit, -jnp.inf, g)
    return sel


def _gate_nt(km, qf):
    kh = km.astype(BF16)
    kl = (km - kh.astype(F32)).astype(BF16)
    qh = qf.astype(BF16)
    ql = (qf - qh.astype(F32)).astype(BF16)
    nb = km.shape[0]
    g1 = _dot_nt(jnp.concatenate([kh, kl], axis=0), qh)
    return g1[:nb] + g1[nb:] + _dot_nt(kh, ql)


def _rank_select(gate, n_past):
    nb = gate.shape[0]
    blk = lax.broadcasted_iota(jnp.int32, gate.shape, 0)
    rank = jnp.zeros(gate.shape, F32)
    for m in range(nb):
        gm = gate[m:m + 1, :]
        beats = (gm > gate) | ((gm == gate) & (blk > m))
        rank = rank + jnp.where(beats & (n_past > m), 1.0, 0.0)
    return jnp.where((blk < n_past) & (rank < MOBA_TOPK), 1.0, 0.0)


def _moba_prompt_kernel(q_ref, k_ref, vt_ref, vt_own_ref, km_ref, o_ref, m_ref, l_ref, acc_ref,
                        *, tq, nb, n_heads):
    i = pl.program_id(1)
    hs = range(n_heads)
    qf = [q_ref[0, g] for g in hs]
    q = [(x * ATTN_SCALE).astype(BF16) for x in qf]
    gate = jnp.concatenate([_gate_nt(km_ref[0, g], qf[g]) for g in hs], axis=1)
    sel = _rank_select(gate, i)
    hcols = lambda g: slice(g * tq, (g + 1) * tq)

    def scores(keys_of):
        return jnp.concatenate([_dot_nt(keys_of(g), q[g]) for g in hs], axis=1)

    def weighted_values(p, vt_of):
        p = p.astype(BF16)
        return jnp.concatenate(
            [jnp.dot(vt_of(g), p[:, hcols(g)], preferred_element_type=F32) for g in hs], axis=1)

    key = lax.broadcasted_iota(jnp.int32, (tq, n_heads * tq), 0)
    qry = lax.broadcasted_iota(jnp.int32, (tq, n_heads * tq), 1) % tq
    d0 = pl.multiple_of(i * tq, tq)
    s = jnp.where(key <= qry, scores(lambda g: k_ref[0, g, pl.ds(d0, tq), :]), NEG_BIG)
    m = jnp.max(s, axis=0, keepdims=True)
    p = jnp.exp(s - m)
    m_ref[...] = m
    l_ref[...] = jnp.sum(p, axis=0, keepdims=True)
    acc_ref[...] = weighted_values(p, lambda g: vt_own_ref[0, g])

    for n in range(nb - 1):
        @pl.when(n < i)
        def _():
            keep = sel[n:n + 1, :] > 0.0
            s = jnp.where(keep, scores(lambda g: k_ref[0, g, n * tq:(n + 1) * tq, :]), NEG_BIG)
            m_old = m_ref[...]
            m_new = jnp.maximum(m_old, jnp.max(s, axis=0, keepdims=True))
            alpha = jnp.exp(m_old - m_new)
            p = jnp.exp(s - m_new)
            m_ref[...] = m_new
            l_ref[...] = alpha * l_ref[...] + jnp.sum(p, axis=0, keepdims=True)
            acc_ref[...] = alpha * acc_ref[...] + weighted_values(
                p, lambda g: vt_ref[0, g, :, n * tq:(n + 1) * tq])

    o = acc_ref[...] / l_ref[...]
    for g in hs:
        o_ref[0, g] = o[:, hcols(g)].T


def _moba_prompt(q_f32, k_bf, vt_bf, kmean):
    B, H, T, _ = q_f32.shape
    tq = MOBA_BLOCK
    nb = T // tq
    q_spec = pl.BlockSpec((1, H, tq, HEAD_DIM), lambda b, i: (b, 0, i, 0))
    return pl.pallas_call(
        functools.partial(_moba_prompt_kernel, tq=tq, nb=nb, n_heads=H),
        grid=(B, nb),
        in_specs=[q_spec,
                  pl.BlockSpec((1, H, T, HEAD_DIM), lambda b, i: (b, 0, 0, 0)),
                  pl.BlockSpec((1, H, HEAD_DIM, T), lambda b, i: (b, 0, 0, 0)),
                  pl.BlockSpec((1, H, HEAD_DIM, tq), lambda b, i: (b, 0, 0, i)),
                  pl.BlockSpec((1, H, nb, HEAD_DIM), lambda b, i: (b, 0, 0, 0))],
        out_specs=q_spec,
        out_shape=jax.ShapeDtypeStruct((B, H, T, HEAD_DIM), F32),
        scratch_shapes=[pltpu.VMEM((1, H * tq), F32), pltpu.VMEM((1, H * tq), F32),
                        pltpu.VMEM((HEAD_DIM, H * tq), F32)],
        compiler_params=_cparams(("parallel", "arbitrary")), name="moba_prompt",
    )(q_f32, k_bf, vt_bf, vt_bf, kmean)


def _mem_attn_kernel(q_ref, k_ref, v_ref, o_ref, *, n_heads):
    for h in range(n_heads):
        cols = slice(h * HEAD_DIM, (h + 1) * HEAD_DIM)
        s = _dot_nt((q_ref[0, h] * ATTN_SCALE).astype(BF16), k_ref[0, :, cols].astype(BF16))
        m = jnp.max(s, axis=1, keepdims=True)
        p = jnp.exp(s - m)
        l = jnp.sum(p, axis=1, keepdims=True)
        o = jnp.dot(p.astype(BF16), v_ref[0, :, cols].astype(BF16), preferred_element_type=F32)
        o_ref[0, h] = o / l


def _mem_attn(q, mk, mv, tq):
    B, H, T, _ = q.shape
    n_mem = mk.shape[1]
    q_spec = pl.BlockSpec((1, H, tq, HEAD_DIM), lambda b, i: (b, 0, i, 0))
    kv_spec = pl.BlockSpec((1, n_mem, H * HEAD_DIM), lambda b, i: (b, 0, 0))
    return pl.pallas_call(
        functools.partial(_mem_attn_kernel, n_heads=H),
        grid=(B, T // tq),
        in_specs=[q_spec, kv_spec, kv_spec],
        out_specs=q_spec,
        out_shape=jax.ShapeDtypeStruct((B, H, T, HEAD_DIM), F32),
        compiler_params=_cparams(("parallel", "arbitrary")), name="mem_attn",
    )(q, mk, mv)


def _merge_kernel(x_ref, oa_ref, ob_ref, om_ref, ga_ref, gb_ref, gm_ref,
                  na_ref, nb_ref, nm_ref, w_ref, lg_ref, lb_ref, y_ref, *, alpha):
    def group(o_ref, g_ref, n_ref):
        o = jnp.concatenate([_heads_to_lanes(o_ref, (bb,)) for bb in range(o_ref.shape[0])],
                            axis=0)
        r = o * lax.rsqrt(jnp.mean(o * o, axis=-1, keepdims=True) + RMS_EPS) * n_ref[...]
        g = g_ref[...]
        return (r * (g * (1.0 / (1.0 + jnp.exp(-g))))).astype(BF16)

    mix = jnp.concatenate([group(oa_ref, ga_ref, na_ref), group(ob_ref, gb_ref, nb_ref),
                           group(om_ref, gm_ref, nm_ref)], axis=1)
    sub = jnp.dot(mix, w_ref[...], preferred_element_type=F32)
    h = alpha * x_ref[...] + sub
    mu = jnp.mean(h, axis=-1, keepdims=True)
    d = h - mu
    var = jnp.mean(d * d, axis=-1, keepdims=True)
    y_ref[...] = d * lax.rsqrt(var + LN_EPS) * lg_ref[...] + lb_ref[...]


def _merge(x2d, o_a, o_b, o_m, g_a, g_b, g_m, norm_a, norm_b, norm_m, w_out_bf, ln_g, ln_b,
           alpha, tm):
    M, D = x2d.shape
    seq = o_a.shape[2]
    tpb = max(seq // tm, 1)
    rpb = min(tm, seq)
    row = lambda a: pl.BlockSpec((tm, a.shape[1]), lambda i: (i, 0))
    head = lambda a: pl.BlockSpec((tm // rpb, a.shape[1], rpb, HEAD_DIM),
                                  lambda i: (i // tpb, 0, i % tpb, 0))
    full = lambda a: pl.BlockSpec(a.shape, lambda i: (0, 0))
    vecs = [v.reshape(1, -1) for v in (norm_a, norm_b, norm_m, ln_g, ln_b)]
    args = [x2d, o_a, o_b, o_m, g_a, g_b, g_m, *vecs[:3], w_out_bf, *vecs[3:]]
    in_specs = ([row(x2d)] + [head(a) for a in (o_a, o_b, o_m)] + [row(a) for a in (g_a, g_b, g_m)]
                + [full(a) for a in args[7:]])
    return pl.pallas_call(
        functools.partial(_merge_kernel, alpha=alpha),
        grid=(M // tm,), in_specs=in_specs,
        out_specs=pl.BlockSpec((tm, D), lambda i: (i, 0)),
        out_shape=jax.ShapeDtypeStruct((M, D), F32),
        compiler_params=_cparams(("parallel",)), name="merge",
    )(*args)


def _block_diag_queries(q):
    B, H, Tq, d = q.shape
    eye = jnp.eye(H, dtype=q.dtype)
    return (q[:, :, :, None, :] * eye[None, :, None, :, None]).reshape(B, H * Tq, H * d)


def _new_token_rows(t):
    B, H, Tq, d = t.shape
    rows = jnp.transpose(t, (0, 2, 1, 3)).reshape(B, Tq, H * d)
    return jnp.pad(rows, ((0, 0), (0, LANES - Tq), (0, 0)))


def _page_rows(page_refs, dtype):
    return jnp.concatenate([_heads_to_lanes(r).astype(dtype) for r in page_refs], axis=0)


def _take_block_diag(acc, o_ref, n_heads, tq):
    for h in range(n_heads):
        o_ref[0, h] = acc[h * tq:(h + 1) * tq, h * HEAD_DIM:(h + 1) * HEAD_DIM]


def _sb_sample_kernel(pt_ref, q_ref, *rest, n_heads, tq, n_steps, n_pg):
    k_refs, v_refs = rest[:n_pg], rest[n_pg:2 * n_pg]
    kn_ref, vn_ref, u_ref, un_ref, o_ref, carry_ref, acc_ref = rest[2 * n_pg:]
    p = pl.program_id(1)
    q = q_ref[0]
    rows = n_heads * tq

    @pl.when(p == 0)
    def _():
        carry_ref[...] = jnp.zeros_like(carry_ref)
        acc_ref[...] = jnp.zeros_like(acc_ref)
        row = lax.broadcasted_iota(jnp.int32, (rows, LANES), 0)
        col = lax.broadcasted_iota(jnp.int32, (rows, LANES), 1)
        _sb_tile(_sb_logits([q], [kn_ref[0].astype(BF16)]), [vn_ref[0].astype(BF16)],
                 un_ref[...], carry_ref, acc_ref, col < row % tq)

    _sb_tile(_sb_logits([q], [_page_rows(k_refs, BF16)]), [_page_rows(v_refs, BF16)],
             u_ref[...], carry_ref, acc_ref, None)

    @pl.when(p == n_steps - 1)
    def _():
        _take_block_diag(acc_ref[...], o_ref, n_heads, tq)


def _sb_sample(q_bd_bf, k_pool, v_pool, k_new_rows, v_new_rows, page_table, tq):
    B, rows, W = q_bd_bf.shape
    _, H, page, _ = k_pool.shape
    n_pg = SAMPLE_PAGES_PER_STEP
    n_steps = page_table.shape[1] // n_pg
    assert n_steps * n_pg == page_table.shape[1]

    def pg(j):
        return pl.BlockSpec((None, H, page, HEAD_DIM),
                            lambda b, p, pt: (pt[b, n_pg * (n_steps - 1 - p) + j], 0, 0, 0))

    per_b = lambda a: pl.BlockSpec((1,) + a.shape[1:], lambda b, p, pt: (b, 0, 0))
    u2, un2 = _suffix_matrix(2 * page), _suffix_matrix(LANES)
    const = lambda a: pl.BlockSpec(a.shape, lambda b, p, pt: (0, 0))
    pages = [pg(j) for j in range(n_pg)]
    grid_spec = pltpu.PrefetchScalarGridSpec(
        num_scalar_prefetch=1, grid=(B, n_steps),
        in_specs=[per_b(q_bd_bf), *pages, *pages, per_b(k_new_rows), per_b(v_new_rows),
                  const(u2), const(un2)],
        out_specs=pl.BlockSpec((1, H, tq, HEAD_DIM), lambda b, p, pt: (b, 0, 0, 0)),
        scratch_shapes=[pltpu.VMEM((rows, LANES), F32), pltpu.VMEM((rows, W), F32)])
    return pl.pallas_call(
        functools.partial(_sb_sample_kernel, n_heads=H, tq=tq, n_steps=n_steps, n_pg=n_pg),
        grid_spec=grid_spec, out_shape=jax.ShapeDtypeStruct((B, H, tq, HEAD_DIM), F32),
        compiler_params=_cparams(("parallel", "arbitrary")), name="sb_sample",
    )(page_table, q_bd_bf, *([k_pool] * n_pg), *([v_pool] * n_pg), k_new_rows, v_new_rows,
      u2, un2)


def _moba_sample_kernel(pt_ref, qf_ref, *rest, n_heads, tq, n_steps, n_pg):
    k_refs, v_refs = rest[:n_pg], rest[n_pg:2 * n_pg]
    kn_ref, vn_ref, o_ref, km_ref, m_ref, l_ref, ob_ref = rest[2 * n_pg:]
    n = pl.program_id(1)
    rows = n_heads * tq
    blk = MOBA_BLOCK
    bps = n_pg // 2
    nb = n_steps * bps
    qf = qf_ref[0]
    q = (qf * ATTN_SCALE).astype(BF16)
    lane = lax.broadcasted_iota(jnp.int32, (rows, LANES), 1)
    diag = lambda o: jnp.concatenate(
        [o[h * tq:(h + 1) * tq, h * HEAD_DIM:(h + 1) * HEAD_DIM] for h in range(n_heads)], axis=0)

    @pl.when(n == 0)
    def _():
        km_ref[...] = jnp.zeros_like(km_ref)
        m_ref[...] = jnp.zeros_like(m_ref)
        l_ref[...] = jnp.zeros_like(l_ref)

    pages = [_heads_to_lanes(r) for r in k_refs]
    kc = jnp.concatenate([pg.astype(BF16) for pg in pages], axis=0)
    s = _dot_nt(q, kc)
    m_all, l_all = m_ref[...], l_ref[...]
    for c in range(bps):
        j = n * bps + c
        km_ref[pl.ds(j, 1), :] = (jnp.sum(pages[2 * c], axis=0, keepdims=True)
                                  + jnp.sum(pages[2 * c + 1], axis=0, keepdims=True)) * (1.0 / blk)
        sc = s[:, c * blk:(c + 1) * blk]
        mc = jnp.max(sc, axis=1, keepdims=True)
        pc = jnp.exp(sc - mc)
        ob_ref[j] = diag(jnp.dot(pc.astype(BF16), _page_rows(v_refs[2 * c:2 * c + 2], BF16),
                                 preferred_element_type=F32))
        m_all = jnp.where(lane == j, mc, m_all)
        l_all = jnp.where(lane == j, jnp.sum(pc, axis=1, keepdims=True), l_all)
    m_ref[...] = m_all
    l_ref[...] = l_all

    @pl.when(n == n_steps - 1)
    def _():
        gate = _dot_nt(qf, km_ref[...], precision=lax.Precision.HIGHEST)
        sel = _select_topk(gate, lane < nb) > 0.0
        row = lax.broadcasted_iota(jnp.int32, (rows, LANES), 0)
        s_new = jnp.where(lane <= row % tq, _dot_nt(q, kn_ref[0].astype(BF16)), NEG_BIG)
        m_blk = m_ref[...]
        m = jnp.maximum(jnp.max(s_new, axis=1, keepdims=True),
                        jnp.max(jnp.where(sel, m_blk, NEG_BIG), axis=1, keepdims=True))
        p_new = jnp.exp(s_new - m)
        w = jnp.where(sel, jnp.exp(jnp.where(sel, m_blk, NEG_BIG) - m), 0.0)
        l = (jnp.sum(p_new, axis=1, keepdims=True)
             + jnp.sum(w * l_ref[...], axis=1, keepdims=True))
        acc = diag(jnp.dot(p_new.astype(BF16), vn_ref[0].astype(BF16),
                           preferred_element_type=F32))
        for j in range(nb):
            acc = acc + w[:, j:j + 1] * ob_ref[j]
        o = acc / l
        for h in range(n_heads):
            o_ref[0, h] = o[h * tq:(h + 1) * tq]


def _moba_sample(q_bd_f32, k_pool, v_pool, k_new_rows, v_new_rows, page_table, tq):
    B, rows, W = q_bd_f32.shape
    _, H, page, _ = k_pool.shape
    assert 2 * page == MOBA_BLOCK
    n_pg = SAMPLE_PAGES_PER_STEP
    n_steps = page_table.shape[1] // n_pg
    assert n_steps * n_pg == page_table.shape[1] and n_pg % 2 == 0
    nb = page_table.shape[1] // 2
    assert nb <= LANES

    def pg(j):
        return pl.BlockSpec((None, H, page, HEAD_DIM),
                            lambda b, n, pt: (pt[b, n_pg * n + j], 0, 0, 0))

    per_b = lambda a: pl.BlockSpec((1,) + a.shape[1:], lambda b, n, pt: (b, 0, 0))
    pages = [pg(j) for j in range(n_pg)]
    grid_spec = pltpu.PrefetchScalarGridSpec(
        num_scalar_prefetch=1, grid=(B, n_steps),
        in_specs=[per_b(q_bd_f32), *pages, *pages, per_b(k_new_rows), per_b(v_new_rows)],
        out_specs=pl.BlockSpec((1, H, tq, HEAD_DIM), lambda b, n, pt: (b, 0, 0, 0)),
        scratch_shapes=[pltpu.VMEM((LANES, W), F32),
                        pltpu.VMEM((rows, LANES), F32),
                        pltpu.VMEM((rows, LANES), F32),
                        pltpu.VMEM((nb, rows, HEAD_DIM), F32)])
    return pl.pallas_call(
        functools.partial(_moba_sample_kernel, n_heads=H, tq=tq, n_steps=n_steps, n_pg=n_pg),
        grid_spec=grid_spec, out_shape=jax.ShapeDtypeStruct((B, H, tq, HEAD_DIM), F32),
        compiler_params=_cparams(("parallel", "arbitrary")), name="moba_sample",
    )(page_table, q_bd_f32, *([k_pool] * n_pg), *([v_pool] * n_pg), k_new_rows, v_new_rows)


def _rope_tables(pos):
    inv = ROPE_THETA ** (-jnp.arange(0, HEAD_DIM, 2, dtype=F32) / HEAD_DIM)
    ang = pos.astype(F32)[:, None] * inv[None, :]
    cos, sin = jnp.cos(ang), jnp.sin(ang)
    return jnp.concatenate([cos, cos], -1), jnp.concatenate([-sin, sin], -1)


def _project_all(x2d, w_in_bf, w_sb, w_moba, w_mem, tm, seq, rope_tabs, prompt):
    assert w_sb == w_moba and (8 * w_sb) % (2 * w_mem) == 0
    grp = 4 * w_sb
    kv = (F32, BF16) if prompt else (F32,)
    seg = lambda k, width, rope=False, rows=(), heads=(), headt=(), km=False, scale=None: (
        k * width, width, rope, rows, heads, headt, km, scale)
    a = _project(x2d, w_in_bf, 0, grp,
                 (seg(0, w_sb, heads=(BF16,) if prompt else (F32,), scale=ATTN_SCALE),
                  seg(1, w_sb, heads=kv),
                  seg(2, w_sb, heads=kv), seg(3, w_sb, rows=(F32,))), tm, seq)
    b = _project(x2d, w_in_bf, 1, grp,
                 (seg(0, w_moba, rope=True, heads=(F32,)),
                  seg(1, w_moba, rope=True, heads=kv, km=prompt),
                  seg(2, w_moba, heads=(F32,), headt=(BF16,) if prompt else ()),
                  seg(3, w_moba, rows=(F32,))), tm, seq, rope_tabs=rope_tabs)
    qm, gm = _project(x2d, w_in_bf, (2 * grp) // (2 * w_mem), 2 * w_mem,
                      (seg(0, w_mem, heads=(F32,)), seg(1, w_mem, rows=(F32,))), tm, seq)
    if prompt:
        names_a = ("qa", "ka", "ka_bf", "va", "va_bf", "ga")
        names_b = ("qb", "kb", "kb_bf", "kmean", "vb", "vbt_bf", "gb")
    else:
        names_a = ("qa", "ka", "va", "ga")
        names_b = ("qb", "kb", "vb", "gb")
    out = dict(zip(names_a, a))
    out.update(zip(names_b, b))
    out.update(qm=qm, gm=gm)
    return out


def kernel(x_prompt, x_sample, cache_sb_k, cache_sb_v, cache_moba_k, cache_moba_v,
           cache_mem_k, cache_mem_v, page_table, mem_prompt,
           w_in, w_mem_k, w_mem_v, norm_a, norm_b, norm_m, w_out, ln_g, ln_b):
    depth = w_in.shape[0]
    B, T, D = x_prompt.shape
    Bs, Ts, _ = x_sample.shape
    h_sb, h_moba, h_mem = cache_sb_k.shape[3], cache_moba_k.shape[3], cache_mem_k.shape[3]
    w_sb, w_moba, w_mem = h_sb * HEAD_DIM, h_moba * HEAD_DIM, h_mem * HEAD_DIM
    n_mem = mem_prompt.shape[1]
    page = cache_sb_k.shape[2]
    past_len = page_table.shape[1] * page
    assert past_len % MOBA_BLOCK == 0 and Ts <= LANES and T % MOBA_BLOCK == 0
    alpha = (2.0 * depth) ** 0.25

    tm = 512
    rope_p = _rope_tables(jnp.arange(T, dtype=jnp.int32))
    rope_s = tuple(jnp.tile(t, (Bs, 1)) for t in
                   _rope_tables(past_len + jnp.arange(Ts, dtype=jnp.int32)))

    y_p = x_prompt.reshape(B * T, D)
    y_s = x_sample.reshape(Bs * Ts, D)
    outs = [[] for _ in range(10)]
    for l in range(depth):
        w_in_bf = w_in[l].astype(BF16)
        w_out_bf = w_out[l].astype(BF16)
        w_memkv_bf = jnp.concatenate([w_mem_k[l], w_mem_v[l]], axis=1).astype(BF16)

        pp = _project_all(y_p, w_in_bf, w_sb, w_moba, w_mem, tm, T, rope_p, True)
        o_a = _sb_prompt(pp["qa"], pp["ka_bf"], pp["va_bf"])
        kmean = jnp.transpose(pp["kmean"], (0, 2, 1, 3, 4)).reshape(
            B, h_moba, T // MOBA_BLOCK, HEAD_DIM)
        o_b = _moba_prompt(pp["qb"], pp["kb_bf"], pp["vbt_bf"], kmean)
        mk, mv = _project(
            mem_prompt.reshape(B * n_mem, D), w_memkv_bf, 0, 2 * w_mem,
            ((0, w_mem, False, (F32,), (), (), False, None),
             (w_mem, w_mem, False, (F32,), (), (), False, None)),
            256, n_mem)
        o_m = _mem_attn(pp["qm"], mk.reshape(B, n_mem, w_mem), mv.reshape(B, n_mem, w_mem), 1024)
        y_p_new = _merge(y_p, o_a, o_b, o_m, pp["ga"], pp["gb"], pp["gm"],
                         norm_a[l], norm_b[l], norm_m[l], w_out_bf, ln_g[l], ln_b[l], alpha, 512)

        ps = _project_all(y_s, w_in_bf, w_sb, w_moba, w_mem, Bs * Ts, Ts, rope_s, False)
        pool = lambda c: jnp.transpose(c[l], (0, 2, 1, 3))
        so_a = _sb_sample(_block_diag_queries(ps["qa"]).astype(BF16), pool(cache_sb_k),
                          pool(cache_sb_v), _new_token_rows(ps["ka"]), _new_token_rows(ps["va"]),
                          page_table, Ts)
        so_b = _moba_sample(_block_diag_queries(ps["qb"]), pool(cache_moba_k),
                            pool(cache_moba_v), _new_token_rows(ps["kb"]),
                            _new_token_rows(ps["vb"]), page_table, Ts)
        so_m = _mem_attn(ps["qm"], cache_mem_k[l].reshape(Bs, n_mem, w_mem),
                         cache_mem_v[l].reshape(Bs, n_mem, w_mem), Ts)
        y_s_new = _merge(y_s, so_a, so_b, so_m, ps["ga"], ps["gb"], ps["gm"],
                         norm_a[l], norm_b[l], norm_m[l], w_out_bf, ln_g[l], ln_b[l], alpha,
                         Bs * Ts)

        tok_major = lambda a: jnp.transpose(a, (0, 2, 1, 3))
        new = [tok_major(pp["ka"]), tok_major(pp["va"]), tok_major(pp["kb"]), tok_major(pp["vb"]),
               mk.reshape(B, n_mem, h_mem, HEAD_DIM), mv.reshape(B, n_mem, h_mem, HEAD_DIM),
               tok_major(ps["ka"]), tok_major(ps["va"]), tok_major(ps["kb"]), tok_major(ps["vb"])]
        for lst, a in zip(outs, new):
            lst.append(a)
        y_p, y_s = y_p_new, y_s_new

    return (y_p.reshape(B, T, D), y_s.reshape(Bs, Ts, D), *[jnp.stack(o) for o in outs])
```

```python
import functools

import jax
import jax.numpy as jnp
import numpy as np
from jax import lax
from jax.experimental import pallas as pl
from jax.experimental.pallas import tpu as pltpu

F32 = jnp.float32
BF16 = jnp.bfloat16

HEAD_DIM = 128
MOBA_BLOCK = 256
MOBA_TOPK = 3
ROPE_THETA = 10000.0
LN_EPS = 1e-5
RMS_EPS = 1e-6
ATTN_SCALE = HEAD_DIM ** -0.5
NEG_BIG = -1e30
LANES = 128
VMEM_LIMIT = 56 * 1024 * 1024
SAMPLE_PAGES_PER_STEP = 16


def _cparams(sem):
    return pltpu.CompilerParams(dimension_semantics=sem, vmem_limit_bytes=VMEM_LIMIT)


def _dot_nt(a, b, precision=None):
    return lax.dot_general(a, b, (((1,), (1,)), ((), ())), precision=precision,
                           preferred_element_type=F32)


def _heads_to_lanes(ref, idx=()):
    n_heads = ref.shape[len(idx)]
    return jnp.concatenate([ref[idx + (h,)] for h in range(n_heads)], axis=1)


def _proj_kernel(*refs, segs, has_rope, tm, seq):
    x_ref, w_ref = refs[0], refs[1]
    pos = 2
    if has_rope:
        cos_ref, sin_ref = refs[2], refs[3]
        pos = 4
    outs = refs[pos:]
    x = x_ref[...].astype(BF16)
    rpb = min(tm, seq)
    o = 0
    for c0, width, rope, row_dtypes, head_dtypes, headt_dtypes, want_kmean, scale in segs:
        r = jnp.dot(x, w_ref[:, c0:c0 + width], preferred_element_type=F32)
        n_heads = width // HEAD_DIM
        if scale is not None:
            r = r * scale
        heads = [r[:, h * HEAD_DIM:(h + 1) * HEAD_DIM] for h in range(n_heads)]
        if rope:
            cos = cos_ref[...]
            sin = sin_ref[...]
            heads = [xh * cos + pltpu.roll(xh, HEAD_DIM // 2, axis=1) * sin for xh in heads]
            r = jnp.concatenate(heads, axis=1)
        for dt in row_dtypes:
            outs[o][...] = r.astype(dt)
            o += 1
        for dt in head_dtypes:
            for bb in range(tm // rpb):
                for h in range(n_heads):
                    outs[o][bb, h] = heads[h][bb * rpb:(bb + 1) * rpb].astype(dt)
            o += 1
        for dt in headt_dtypes:
            for h in range(n_heads):
                outs[o][0, h] = heads[h].T.astype(dt)
            o += 1
        if want_kmean:
            nblk = tm // MOBA_BLOCK
            for h in range(n_heads):
                outs[o][0, 0, h] = jnp.sum(heads[h].reshape(nblk, MOBA_BLOCK, HEAD_DIM),
                                           axis=1) * (1.0 / MOBA_BLOCK)
            o += 1


def _project(x2d, w_bf, col_block, col_width, segs, tm, seq, rope_tabs=None):
    M, D = x2d.shape
    nt = M // tm
    n_batch = M // seq
    tpb = max(seq // tm, 1)
    rpb = min(tm, seq)
    has_rope = rope_tabs is not None
    in_specs = [pl.BlockSpec((tm, D), lambda i: (i, 0)),
                pl.BlockSpec((D, col_width), lambda i: (0, col_block))]
    args = [x2d, w_bf]
    if has_rope:
        in_specs += [pl.BlockSpec((tm, HEAD_DIM), lambda i: (i % tpb, 0))] * 2
        args += list(rope_tabs)
    out_shapes, out_specs = [], []
    for c0, width, rope, row_dtypes, head_dtypes, headt_dtypes, want_kmean, scale in segs:
        n_heads = width // HEAD_DIM
        for dt in row_dtypes:
            out_shapes.append(jax.ShapeDtypeStruct((M, width), dt))
            out_specs.append(pl.BlockSpec((tm, width), lambda i: (i, 0)))
        for dt in head_dtypes:
            out_shapes.append(jax.ShapeDtypeStruct((n_batch, n_heads, seq, HEAD_DIM), dt))
            out_specs.append(pl.BlockSpec((tm // rpb, n_heads, rpb, HEAD_DIM),
                                          lambda i: (i // tpb, 0, i % tpb, 0)))
        for dt in headt_dtypes:
            assert tm <= seq
            out_shapes.append(jax.ShapeDtypeStruct((n_batch, n_heads, HEAD_DIM, seq), dt))
            out_specs.append(pl.BlockSpec((1, n_heads, HEAD_DIM, tm),
                                          lambda i: (i // tpb, 0, 0, i % tpb)))
        if want_kmean:
            nblk = tm // MOBA_BLOCK
            out_shapes.append(jax.ShapeDtypeStruct((n_batch, tpb, n_heads, nblk, HEAD_DIM), F32))
            out_specs.append(pl.BlockSpec((1, 1, n_heads, nblk, HEAD_DIM),
                                          lambda i: (i // tpb, i % tpb, 0, 0, 0)))
    return pl.pallas_call(
        functools.partial(_proj_kernel, segs=segs, has_rope=has_rope, tm=tm, seq=seq),
        grid=(nt,), in_specs=in_specs, out_specs=out_specs, out_shape=out_shapes,
        compiler_params=_cparams(("parallel",)), name="proj",
    )(*args)


def _sb_logits(qs, kts):
    return jnp.concatenate([_dot_nt(q, kt) for q, kt in zip(qs, kts)], axis=0)


def _sb_tile(z, vts, u2, carry_ref, acc_ref, mask):
    ck = u2.shape[1]
    nch = z.shape[1] // ck
    rows = z.shape[0]
    rows_h = rows // len(vts)

    def mask_newest(x):
        if mask is None:
            return x
        newest = jnp.where(mask, x[:, (nch - 1) * ck:], 0.0)
        return jnp.concatenate([x[:, :(nch - 1) * ck], newest], axis=1) if nch > 1 else newest

    sp = mask_newest(jnp.maximum(z, 0.0) + jnp.log(1.0 + jnp.exp(-jnp.abs(z))))
    st = (jnp.concatenate([sp[:, c * ck:(c + 1) * ck] for c in range(nch)], axis=0)
          if nch > 1 else sp)
    tail = jnp.dot(st.astype(BF16), u2, preferred_element_type=F32)
    carry = carry_ref[...]
    reps = ck // LANES
    a_parts = [None] * nch
    for c in reversed(range(nch)):
        tail_c = tail[c * rows:(c + 1) * rows]
        a_parts[c] = jnp.exp(z[:, c * ck:(c + 1) * ck] - tail_c - jnp.tile(carry, (1, reps)))
        carry = carry + tail_c[:, :1]
    a = mask_newest(jnp.concatenate(a_parts, axis=1) if nch > 1 else a_parts[0]).astype(BF16)
    for g, vt in enumerate(vts):
        sl = slice(g * rows_h, (g + 1) * rows_h)
        acc_ref[sl, :] += jnp.dot(a[sl], vt, preferred_element_type=F32)
    carry_ref[...] = carry


def _sb_prompt_kernel(q_ref, k_ref, v_ref, u_ref, o_ref, carry_ref, acc_ref, *, tq, n_grp):
    i = pl.program_id(2)
    u2 = u_ref[...]
    carry_ref[...] = jnp.zeros_like(carry_ref)
    acc_ref[...] = jnp.zeros_like(acc_ref)
    row = lax.broadcasted_iota(jnp.int32, (n_grp * tq, tq), 0)
    col = lax.broadcasted_iota(jnp.int32, (n_grp * tq, tq), 1)
    qs = [q_ref[0, g] for g in range(n_grp)]

    def logits(tile_idx):
        k0 = pl.multiple_of(tile_idx * tq, tq)
        return _sb_logits(qs, [k_ref[0, g, pl.ds(k0, tq), :] for g in range(n_grp)])

    def values(tile_idx):
        k0 = pl.multiple_of(tile_idx * tq, tq)
        return [v_ref[0, g, pl.ds(k0, tq), :] for g in range(n_grp)]

    _sb_tile(logits(i), values(i), u2, carry_ref, acc_ref, col < row % tq)

    def body(s, c):
        j = i - 1 - s
        _sb_tile(logits(j), values(j), u2, carry_ref, acc_ref, None)
        return c

    lax.fori_loop(0, i, body, 0)
    for g in range(n_grp):
        o_ref[0, g] = acc_ref[g * tq:(g + 1) * tq, :]


def _suffix_matrix(n):
    u = (np.arange(n)[:, None] >= np.arange(n)[None, :]).astype(np.float32)
    return jnp.asarray(u, dtype=BF16)


def _sb_prompt(q_bf, k_bf, v_bf, tq=256, n_grp=6):
    B, H, T, _ = q_bf.shape
    assert H % n_grp == 0
    q_spec = pl.BlockSpec((1, n_grp, tq, HEAD_DIM), lambda b, h, i: (b, h, i, 0))
    kv_spec = pl.BlockSpec((1, n_grp, T, HEAD_DIM), lambda b, h, i: (b, h, 0, 0))
    return pl.pallas_call(
        functools.partial(_sb_prompt_kernel, tq=tq, n_grp=n_grp),
        grid=(B, H // n_grp, T // tq),
        in_specs=[q_spec, kv_spec, kv_spec, pl.BlockSpec((tq, tq), lambda b, h, i: (0, 0))],
        out_specs=q_spec,
        out_shape=jax.ShapeDtypeStruct((B, H, T, HEAD_DIM), F32),
        scratch_shapes=[pltpu.VMEM((n_grp * tq, LANES), F32),
                        pltpu.VMEM((n_grp * tq, HEAD_DIM), F32)],
        compiler_params=_cparams(("parallel", "parallel", "arbitrary")), name="sb_prompt",
    )(q_bf, k_bf, v_bf, _suffix_matrix(tq))


def _select_topk(gate, past):
    lane = lax.broadcasted_iota(jnp.int32, gate.shape, 1)
    g = jnp.where(past, gate, -jnp.inf)
    sel = jnp.zeros(gate.shape, F32)
    for _ in range(MOBA_TOPK):
        m = jnp.max(g, axis=1, keepdims=True)
        idx = jnp.min(jnp.where(g == m, lane, LANES), axis=1, keepdims=True)
        hit = lane == idx
        sel = jnp.where(hit & (m > -jnp.inf), 1.0, sel)
        g = jnp.where(hit, -jnp.inf, g)
    return sel


def _gate_nt(km, qf):
    kh = km.astype(BF16)
    kl = (km - kh.astype(F32)).astype(BF16)
    qh = qf.astype(BF16)
    ql = (qf - qh.astype(F32)).astype(BF16)
    nb = km.shape[0]
    g1 = _dot_nt(jnp.concatenate([kh, kl], axis=0), qh)
    return g1[:nb] + g1[nb:] + _dot_nt(kh, ql)


def _rank_select(gate, n_past):
    nb = gate.shape[0]
    blk = lax.broadcasted_iota(jnp.int32, gate.shape, 0)
    rank = jnp.zeros(gate.shape, F32)
    for m in range(nb):
        gm = gate[m:m + 1, :]
        beats = (gm > gate) | ((gm == gate) & (blk > m))
        rank = rank + jnp.where(beats & (n_past > m), 1.0, 0.0)
    return jnp.where((blk < n_past) & (rank < MOBA_TOPK), 1.0, 0.0)


def _moba_prompt_kernel(q_ref, k_ref, vt_ref, km_ref, o_ref, *, tq, nb, n_heads):
    i = pl.program_id(1)
    hs = range(n_heads)
    qf = [q_ref[0, g] for g in hs]
    q = [(x * ATTN_SCALE).astype(BF16) for x in qf]
    gate = jnp.concatenate([_gate_nt(km_ref[0, g], qf[g]) for g in hs], axis=1)
    sel = _rank_select(gate, i)
    hcols = lambda g: slice(g * tq, (g + 1) * tq)
    key = lax.broadcasted_iota(jnp.int32, (tq, n_heads * tq), 0)
    qry = lax.broadcasted_iota(jnp.int32, (tq, n_heads * tq), 1) % tq

    def attend(nq):
        nk = (nq + 1) * tq
        s = jnp.concatenate([_dot_nt(k_ref[0, g, :nk, :], q[g]) for g in hs], axis=1)
        blocks = [jnp.where(sel[n:n + 1, :] > 0.0, s[n * tq:(n + 1) * tq], NEG_BIG)
                  for n in range(nq)]
        blocks.append(jnp.where(key <= qry, s[nq * tq:], NEG_BIG))
        s = jnp.concatenate(blocks, axis=0) if nq else blocks[0]
        m = jnp.max(s, axis=0, keepdims=True)
        p = jnp.exp(s - m)
        l = jnp.sum(p, axis=0, keepdims=True)
        p = p.astype(BF16)
        o = jnp.concatenate(
            [jnp.dot(vt_ref[0, g, :, :nk], p[:, hcols(g)], preferred_element_type=F32)
             for g in hs], axis=1) / l
        for g in hs:
            o_ref[0, g] = o[:, hcols(g)].T

    for nq in range(nb):
        pl.when(i == nq)(functools.partial(attend, nq))


def _moba_prompt(q_f32, k_bf, vt_bf, kmean):
    B, H, T, _ = q_f32.shape
    tq = MOBA_BLOCK
    nb = T // tq
    q_spec = pl.BlockSpec((1, H, tq, HEAD_DIM), lambda b, i: (b, 0, i, 0))
    return pl.pallas_call(
        functools.partial(_moba_prompt_kernel, tq=tq, nb=nb, n_heads=H),
        grid=(B, nb),
        in_specs=[q_spec,
                  pl.BlockSpec((1, H, T, HEAD_DIM), lambda b, i: (b, 0, 0, 0)),
                  pl.BlockSpec((1, H, HEAD_DIM, T), lambda b, i: (b, 0, 0, 0)),
                  pl.BlockSpec((1, H, nb, HEAD_DIM), lambda b, i: (b, 0, 0, 0))],
        out_specs=q_spec,
        out_shape=jax.ShapeDtypeStruct((B, H, T, HEAD_DIM), F32),
        compiler_params=_cparams(("parallel", "arbitrary")), name="moba_prompt",
    )(q_f32, k_bf, vt_bf, kmean)


def _mem_attn_kernel(q_ref, k_ref, v_ref, o_ref, *, n_heads):
    for h in range(n_heads):
        cols = slice(h * HEAD_DIM, (h + 1) * HEAD_DIM)
        s = _dot_nt((q_ref[0, h] * ATTN_SCALE).astype(BF16), k_ref[0, :, cols].astype(BF16))
        m = jnp.max(s, axis=1, keepdims=True)
        p = jnp.exp(s - m)
        l = jnp.sum(p, axis=1, keepdims=True)
        o = jnp.dot(p.astype(BF16), v_ref[0, :, cols].astype(BF16), preferred_element_type=F32)
        o_ref[0, h] = o / l


def _mem_attn(q, mk, mv, tq):
    B, H, T, _ = q.shape
    n_mem = mk.shape[1]
    q_spec = pl.BlockSpec((1, H, tq, HEAD_DIM), lambda b, i: (b, 0, i, 0))
    kv_spec = pl.BlockSpec((1, n_mem, H * HEAD_DIM), lambda b, i: (b, 0, 0))
    return pl.pallas_call(
        functools.partial(_mem_attn_kernel, n_heads=H),
        grid=(B, T // tq),
        in_specs=[q_spec, kv_spec, kv_spec],
        out_specs=q_spec,
        out_shape=jax.ShapeDtypeStruct((B, H, T, HEAD_DIM), F32),
        compiler_params=_cparams(("parallel", "arbitrary")), name="mem_attn",
    )(q, mk, mv)


def _merge_kernel(x_ref, oa_ref, ob_ref, om_ref, ga_ref, gb_ref, gm_ref,
                  na_ref, nb_ref, nm_ref, w_ref, lg_ref, lb_ref, y_ref, *, alpha):
    def group(o_ref, g_ref, n_ref):
        o = jnp.concatenate([_heads_to_lanes(o_ref, (bb,)) for bb in range(o_ref.shape[0])],
                            axis=0)
        r = o * lax.rsqrt(jnp.mean(o * o, axis=-1, keepdims=True) + RMS_EPS) * n_ref[...]
        g = g_ref[...]
        return (r * (g * (1.0 / (1.0 + jnp.exp(-g))))).astype(BF16)

    mix = jnp.concatenate([group(oa_ref, ga_ref, na_ref), group(ob_ref, gb_ref, nb_ref),
                           group(om_ref, gm_ref, nm_ref)], axis=1)
    sub = jnp.dot(mix, w_ref[...], preferred_element_type=F32)
    h = alpha * x_ref[...] + sub
    mu = jnp.mean(h, axis=-1, keepdims=True)
    d = h - mu
    var = jnp.mean(d * d, axis=-1, keepdims=True)
    y_ref[...] = d * lax.rsqrt(var + LN_EPS) * lg_ref[...] + lb_ref[...]


def _merge(x2d, o_a, o_b, o_m, g_a, g_b, g_m, norm_a, norm_b, norm_m, w_out_bf, ln_g, ln_b,
           alpha, tm):
    M, D = x2d.shape
    seq = o_a.shape[2]
    tpb = max(seq // tm, 1)
    rpb = min(tm, seq)
    row = lambda a: pl.BlockSpec((tm, a.shape[1]), lambda i: (i, 0))
    head = lambda a: pl.BlockSpec((tm // rpb, a.shape[1], rpb, HEAD_DIM),
                                  lambda i: (i // tpb, 0, i % tpb, 0))
    full = lambda a: pl.BlockSpec(a.shape, lambda i: (0, 0))
    vecs = [v.reshape(1, -1) for v in (norm_a, norm_b, norm_m, ln_g, ln_b)]
    args = [x2d, o_a, o_b, o_m, g_a, g_b, g_m, *vecs[:3], w_out_bf, *vecs[3:]]
    in_specs = ([row(x2d)] + [head(a) for a in (o_a, o_b, o_m)] + [row(a) for a in (g_a, g_b, g_m)]
                + [full(a) for a in args[7:]])
    return pl.pallas_call(
        functools.partial(_merge_kernel, alpha=alpha),
        grid=(M // tm,), in_specs=in_specs,
        out_specs=pl.BlockSpec((tm, D), lambda i: (i, 0)),
        out_shape=jax.ShapeDtypeStruct((M, D), F32),
        compiler_params=_cparams(("parallel",)), name="merge",
    )(*args)


def _block_diag_queries(q):
    B, H, Tq, d = q.shape
    eye = jnp.eye(H, dtype=q.dtype)
    return (q[:, :, :, None, :] * eye[None, :, None, :, None]).reshape(B, H * Tq, H * d)


def _new_token_rows(t):
    B, H, Tq, d = t.shape
    rows = jnp.transpose(t, (0, 2, 1, 3)).reshape(B, Tq, H * d)
    return jnp.pad(rows, ((0, 0), (0, LANES - Tq), (0, 0)))


def _page_rows(page_refs, dtype):
    return jnp.concatenate([_heads_to_lanes(r).astype(dtype) for r in page_refs], axis=0)


def _take_block_diag(acc, o_ref, n_heads, tq):
    for h in range(n_heads):
        o_ref[0, h] = acc[h * tq:(h + 1) * tq, h * HEAD_DIM:(h + 1) * HEAD_DIM]


def _sb_sample_kernel(pt_ref, q_ref, *rest, n_heads, tq, n_steps, n_pg):
    k_refs, v_refs = rest[:n_pg], rest[n_pg:2 * n_pg]
    kn_ref, vn_ref, u_ref, un_ref, o_ref, carry_ref, acc_ref = rest[2 * n_pg:]
    p = pl.program_id(1)
    q = q_ref[0]
    rows = n_heads * tq

    @pl.when(p == 0)
    def _():
        carry_ref[...] = jnp.zeros_like(carry_ref)
        acc_ref[...] = jnp.zeros_like(acc_ref)
        row = lax.broadcasted_iota(jnp.int32, (rows, LANES), 0)
        col = lax.broadcasted_iota(jnp.int32, (rows, LANES), 1)
        _sb_tile(_sb_logits([q], [kn_ref[0].astype(BF16)]), [vn_ref[0].astype(BF16)],
                 un_ref[...], carry_ref, acc_ref, col < row % tq)

    _sb_tile(_sb_logits([q], [_page_rows(k_refs, BF16)]), [_page_rows(v_refs, BF16)],
             u_ref[...], carry_ref, acc_ref, None)

    @pl.when(p == n_steps - 1)
    def _():
        _take_block_diag(acc_ref[...], o_ref, n_heads, tq)


def _sb_sample(q_bd_bf, k_pool, v_pool, k_new_rows, v_new_rows, page_table, tq):
    B, rows, W = q_bd_bf.shape
    _, H, page, _ = k_pool.shape
    n_pg = SAMPLE_PAGES_PER_STEP
    n_steps = page_table.shape[1] // n_pg
    assert n_steps * n_pg == page_table.shape[1]

    def pg(j):
        return pl.BlockSpec((None, H, page, HEAD_DIM),
                            lambda b, p, pt: (pt[b, n_pg * (n_steps - 1 - p) + j], 0, 0, 0))

    per_b = lambda a: pl.BlockSpec((1,) + a.shape[1:], lambda b, p, pt: (b, 0, 0))
    u2, un2 = _suffix_matrix(2 * page), _suffix_matrix(LANES)
    const = lambda a: pl.BlockSpec(a.shape, lambda b, p, pt: (0, 0))
    pages = [pg(j) for j in range(n_pg)]
    grid_spec = pltpu.PrefetchScalarGridSpec(
        num_scalar_prefetch=1, grid=(B, n_steps),
        in_specs=[per_b(q_bd_bf), *pages, *pages, per_b(k_new_rows), per_b(v_new_rows),
                  const(u2), const(un2)],
        out_specs=pl.BlockSpec((1, H, tq, HEAD_DIM), lambda b, p, pt: (b, 0, 0, 0)),
        scratch_shapes=[pltpu.VMEM((rows, LANES), F32), pltpu.VMEM((rows, W), F32)])
    return pl.pallas_call(
        functools.partial(_sb_sample_kernel, n_heads=H, tq=tq, n_steps=n_steps, n_pg=n_pg),
        grid_spec=grid_spec, out_shape=jax.ShapeDtypeStruct((B, H, tq, HEAD_DIM), F32),
        compiler_params=_cparams(("parallel", "arbitrary")), name="sb_sample",
    )(page_table, q_bd_bf, *([k_pool] * n_pg), *([v_pool] * n_pg), k_new_rows, v_new_rows,
      u2, un2)


def _moba_sample_kernel(pt_ref, qf_ref, *rest, n_heads, tq, n_steps, n_pg):
    k_refs, v_refs = rest[:n_pg], rest[n_pg:2 * n_pg]
    kn_ref, vn_ref, o_ref, km_ref, m_ref, l_ref, ob_ref = rest[2 * n_pg:]
    n = pl.program_id(1)
    rows = n_heads * tq
    blk = MOBA_BLOCK
    bps = n_pg // 2
    nb = n_steps * bps
    qf = qf_ref[0]
    q = (qf * ATTN_SCALE).astype(BF16)
    lane = lax.broadcasted_iota(jnp.int32, (rows, LANES), 1)
    diag = lambda o: jnp.concatenate(
        [o[h * tq:(h + 1) * tq, h * HEAD_DIM:(h + 1) * HEAD_DIM] for h in range(n_heads)], axis=0)

    @pl.when(n == 0)
    def _():
        km_ref[...] = jnp.zeros_like(km_ref)
        m_ref[...] = jnp.zeros_like(m_ref)
        l_ref[...] = jnp.zeros_like(l_ref)

    pages = [_heads_to_lanes(r) for r in k_refs]
    kc = jnp.concatenate([pg.astype(BF16) for pg in pages], axis=0)
    s = _dot_nt(q, kc)
    m_all, l_all = m_ref[...], l_ref[...]
    for c in range(bps):
        j = n * bps + c
        km_ref[pl.ds(j, 1), :] = (jnp.sum(pages[2 * c], axis=0, keepdims=True)
                                  + jnp.sum(pages[2 * c + 1], axis=0, keepdims=True)) * (1.0 / blk)
        sc = s[:, c * blk:(c + 1) * blk]
        mc = jnp.max(sc, axis=1, keepdims=True)
        pc = jnp.exp(sc - mc)
        ob_ref[j] = diag(jnp.dot(pc.astype(BF16), _page_rows(v_refs[2 * c:2 * c + 2], BF16),
                                 preferred_element_type=F32))
        m_all = jnp.where(lane == j, mc, m_all)
        l_all = jnp.where(lane == j, jnp.sum(pc, axis=1, keepdims=True), l_all)
    m_ref[...] = m_all
    l_ref[...] = l_all

    @pl.when(n == n_steps - 1)
    def _():
        gate = _dot_nt(qf, km_ref[...], precision=lax.Precision.HIGHEST)
        sel = _select_topk(gate, lane < nb) > 0.0
        row = lax.broadcasted_iota(jnp.int32, (rows, LANES), 0)
        s_new = jnp.where(lane <= row % tq, _dot_nt(q, kn_ref[0].astype(BF16)), NEG_BIG)
        m_blk = m_ref[...]
        m = jnp.maximum(jnp.max(s_new, axis=1, keepdims=True),
                        jnp.max(jnp.where(sel, m_blk, NEG_BIG), axis=1, keepdims=True))
        p_new = jnp.exp(s_new - m)
        w = jnp.where(sel, jnp.exp(jnp.where(sel, m_blk, NEG_BIG) - m), 0.0)
        l = (jnp.sum(p_new, axis=1, keepdims=True)
             + jnp.sum(w * l_ref[...], axis=1, keepdims=True))
        acc = diag(jnp.dot(p_new.astype(BF16), vn_ref[0].astype(BF16),
                           preferred_element_type=F32))
        for j in range(nb):
            acc = acc + w[:, j:j + 1] * ob_ref[j]
        o = acc / l
        for h in range(n_heads):
            o_ref[0, h] = o[h * tq:(h + 1) * tq]


def _moba_sample(q_bd_f32, k_pool, v_pool, k_new_rows, v_new_rows, page_table, tq):
    B, rows, W = q_bd_f32.shape
    _, H, page, _ = k_pool.shape
    assert 2 * page == MOBA_BLOCK
    n_pg = SAMPLE_PAGES_PER_STEP
    n_steps = page_table.shape[1] // n_pg
    assert n_steps * n_pg == page_table.shape[1] and n_pg % 2 == 0
    nb = page_table.shape[1] // 2
    assert nb <= LANES

    def pg(j):
        return pl.BlockSpec((None, H, page, HEAD_DIM),
                            lambda b, n, pt: (pt[b, n_pg * n + j], 0, 0, 0))

    per_b = lambda a: pl.BlockSpec((1,) + a.shape[1:], lambda b, n, pt: (b, 0, 0))
    pages = [pg(j) for j in range(n_pg)]
    grid_spec = pltpu.PrefetchScalarGridSpec(
        num_scalar_prefetch=1, grid=(B, n_steps),
        in_specs=[per_b(q_bd_f32), *pages, *pages, per_b(k_new_rows), per_b(v_new_rows)],
        out_specs=pl.BlockSpec((1, H, tq, HEAD_DIM), lambda b, n, pt: (b, 0, 0, 0)),
        scratch_shapes=[pltpu.VMEM((LANES, W), F32),
                        pltpu.VMEM((rows, LANES), F32),
                        pltpu.VMEM((rows, LANES), F32),
                        pltpu.VMEM((nb, rows, HEAD_DIM), F32)])
    return pl.pallas_call(
        functools.partial(_moba_sample_kernel, n_heads=H, tq=tq, n_steps=n_steps, n_pg=n_pg),
        grid_spec=grid_spec, out_shape=jax.ShapeDtypeStruct((B, H, tq, HEAD_DIM), F32),
        compiler_params=_cparams(("parallel", "arbitrary")), name="moba_sample",
    )(page_table, q_bd_f32, *([k_pool] * n_pg), *([v_pool] * n_pg), k_new_rows, v_new_rows)


def _rope_tables(pos):
    inv = ROPE_THETA ** (-jnp.arange(0, HEAD_DIM, 2, dtype=F32) / HEAD_DIM)
    ang = pos.astype(F32)[:, None] * inv[None, :]
    cos, sin = jnp.cos(ang), jnp.sin(ang)
    return jnp.concatenate([cos, cos], -1), jnp.concatenate([-sin, sin], -1)


def _project_all(x2d, w_in_bf, w_sb, w_moba, w_mem, tm, seq, rope_tabs, prompt):
    assert w_sb == w_moba and (8 * w_sb) % (2 * w_mem) == 0
    grp = 4 * w_sb
    kv = (F32, BF16) if prompt else (F32,)
    seg = lambda k, width, rope=False, rows=(), heads=(), headt=(), km=False, scale=None: (
        k * width, width, rope, rows, heads, headt, km, scale)
    a = _project(x2d, w_in_bf, 0, grp,
                 (seg(0, w_sb, heads=(BF16,) if prompt else (F32,), scale=ATTN_SCALE),
                  seg(1, w_sb, heads=kv),
                  seg(2, w_sb, heads=kv), seg(3, w_sb, rows=(F32,))), tm, seq)
    b = _project(x2d, w_in_bf, 1, grp,
                 (seg(0, w_moba, rope=True, heads=(F32,)),
                  seg(1, w_moba, rope=True, heads=kv, km=prompt),
                  seg(2, w_moba, heads=(F32,), headt=(BF16,) if prompt else ()),
                  seg(3, w_moba, rows=(F32,))), tm, seq, rope_tabs=rope_tabs)
    qm, gm = _project(x2d, w_in_bf, (2 * grp) // (2 * w_mem), 2 * w_mem,
                      (seg(0, w_mem, heads=(F32,)), seg(1, w_mem, rows=(F32,))), tm, seq)
    if prompt:
        names_a = ("qa", "ka", "ka_bf", "va", "va_bf", "ga")
        names_b = ("qb", "kb", "kb_bf", "kmean", "vb", "vbt_bf", "gb")
    else:
        names_a = ("qa", "ka", "va", "ga")
        names_b = ("qb", "kb", "vb", "gb")
    out = dict(zip(names_a, a))
    out.update(zip(names_b, b))
    out.update(qm=qm, gm=gm)
    return out


def kernel(x_prompt, x_sample, cache_sb_k, cache_sb_v, cache_moba_k, cache_moba_v,
           cache_mem_k, cache_mem_v, page_table, mem_prompt,
           w_in, w_mem_k, w_mem_v, norm_a, norm_b, norm_m, w_out, ln_g, ln_b):
    depth = w_in.shape[0]
    B, T, D = x_prompt.shape
    Bs, Ts, _ = x_sample.shape
    h_sb, h_moba, h_mem = cache_sb_k.shape[3], cache_moba_k.shape[3], cache_mem_k.shape[3]
    w_sb, w_moba, w_mem = h_sb * HEAD_DIM, h_moba * HEAD_DIM, h_mem * HEAD_DIM
    n_mem = mem_prompt.shape[1]
    page = cache_sb_k.shape[2]
    past_len = page_table.shape[1] * page
    assert past_len % MOBA_BLOCK == 0 and Ts <= LANES and T % MOBA_BLOCK == 0
    alpha = (2.0 * depth) ** 0.25

    tm = 512
    rope_p = _rope_tables(jnp.arange(T, dtype=jnp.int32))
    rope_s = tuple(jnp.tile(t, (Bs, 1)) for t in
                   _rope_tables(past_len + jnp.arange(Ts, dtype=jnp.int32)))

    y_p = x_prompt.reshape(B * T, D)
    y_s = x_sample.reshape(Bs * Ts, D)
    outs = [[] for _ in range(10)]
    for l in range(depth):
        w_in_bf = w_in[l].astype(BF16)
        w_out_bf = w_out[l].astype(BF16)
        w_memkv_bf = jnp.concatenate([w_mem_k[l], w_mem_v[l]], axis=1).astype(BF16)

        pp = _project_all(y_p, w_in_bf, w_sb, w_moba, w_mem, tm, T, rope_p, True)
        o_a = _sb_prompt(pp["qa"], pp["ka_bf"], pp["va_bf"])
        kmean = jnp.transpose(pp["kmean"], (0, 2, 1, 3, 4)).reshape(
            B, h_moba, T // MOBA_BLOCK, HEAD_DIM)
        o_b = _moba_prompt(pp["qb"], pp["kb_bf"], pp["vbt_bf"], kmean)
        mk, mv = _project(
            mem_prompt.reshape(B * n_mem, D), w_memkv_bf, 0, 2 * w_mem,
            ((0, w_mem, False, (F32,), (), (), False, None),
             (w_mem, w_mem, False, (F32,), (), (), False, None)),
            256, n_mem)
        o_m = _mem_attn(pp["qm"], mk.reshape(B, n_mem, w_mem), mv.reshape(B, n_mem, w_mem), 1024)
        y_p_new = _merge(y_p, o_a, o_b, o_m, pp["ga"], pp["gb"], pp["gm"],
                         norm_a[l], norm_b[l], norm_m[l], w_out_bf, ln_g[l], ln_b[l], alpha, 512)

        ps = _project_all(y_s, w_in_bf, w_sb, w_moba, w_mem, Bs * Ts, Ts, rope_s, False)
        pool = lambda c: jnp.transpose(c[l], (0, 2, 1, 3))
        so_a = _sb_sample(_block_diag_queries(ps["qa"]).astype(BF16), pool(cache_sb_k),
                          pool(cache_sb_v), _new_token_rows(ps["ka"]), _new_token_rows(ps["va"]),
                          page_table, Ts)
        so_b = _moba_sample(_block_diag_queries(ps["qb"]), pool(cache_moba_k),
                            pool(cache_moba_v), _new_token_rows(ps["kb"]),
                            _new_token_rows(ps["vb"]), page_table, Ts)
        so_m = _mem_attn(ps["qm"], cache_mem_k[l].reshape(Bs, n_mem, w_mem),
                         cache_mem_v[l].reshape(Bs, n_mem, w_mem), Ts)
        y_s_new = _merge(y_s, so_a, so_b, so_m, ps["ga"], ps["gb"], ps["gm"],
                         norm_a[l], norm_b[l], norm_m[l], w_out_bf, ln_g[l], ln_b[l], alpha,
                         Bs * Ts)

        tok_major = lambda a: jnp.transpose(a, (0, 2, 1, 3))
        new = [tok_major(pp["ka"]), tok_major(pp["va"]), tok_major(pp["kb"]), tok_major(pp["vb"]),
               mk.reshape(B, n_mem, h_mem, HEAD_DIM), mv.reshape(B, n_mem, h_mem, HEAD_DIM),
               tok_major(ps["ka"]), tok_major(ps["va"]), tok_major(ps["kb"]), tok_major(ps["vb"])]
        for lst, a in zip(outs, new):
            lst.append(a)
        y_p, y_s = y_p_new, y_s_new

    return (y_p.reshape(B, T, D), y_s.reshape(Bs, Ts, D), *[jnp.stack(o) for o in outs])
```

```python
import functools

import jax
import jax.numpy as jnp
import numpy as np
from jax import lax
from jax.experimental import pallas as pl
from jax.experimental.pallas import tpu as pltpu

F32 = jnp.float32
BF16 = jnp.bfloat16

HEAD_DIM = 128
MOBA_BLOCK = 256
MOBA_TOPK = 3
ROPE_THETA = 10000.0
LN_EPS = 1e-5
RMS_EPS = 1e-6
ATTN_SCALE = HEAD_DIM ** -0.5
NEG_BIG = -1e30
LANES = 128
VMEM_LIMIT = 56 * 1024 * 1024
SAMPLE_PAGES_PER_STEP = 16


def _cparams(sem):
    return pltpu.CompilerParams(dimension_semantics=sem, vmem_limit_bytes=VMEM_LIMIT)


def _dot_nt(a, b, precision=None):
    return lax.dot_general(a, b, (((1,), (1,)), ((), ())), precision=precision,
                           preferred_element_type=F32)


def _heads_to_lanes(ref, idx=()):
    n_heads = ref.shape[len(idx)]
    return jnp.concatenate([ref[idx + (h,)] for h in range(n_heads)], axis=1)


def _proj_kernel(*refs, segs, has_rope, tm, seq, interleave):
    x_ref, w_ref = refs[0], refs[1]
    pos = 2
    if has_rope:
        cos_ref, sin_ref = refs[2], refs[3]
        pos = 4
    outs = refs[pos:]
    x = x_ref[...].astype(BF16)
    rpb = min(tm, seq)
    o = 0
    for c0, width, rope, row_dtypes, head_dtypes, headt_dtypes, want_kmean, scale in segs:
        r = jnp.dot(x, w_ref[:, c0:c0 + width], preferred_element_type=F32)
        n_heads = width // HEAD_DIM
        if scale is not None:
            r = r * scale
        heads = [r[:, h * HEAD_DIM:(h + 1) * HEAD_DIM] for h in range(n_heads)]
        if rope:
            cos = cos_ref[...]
            sin = sin_ref[...]
            heads = [xh * cos + pltpu.roll(xh, HEAD_DIM // 2, axis=1) * sin for xh in heads]
            r = jnp.concatenate(heads, axis=1)
        for dt in row_dtypes:
            if interleave:
                for h in range(n_heads):
                    outs[o][pl.ds(h, tm, stride=n_heads), :] = heads[h].astype(dt)
            else:
                outs[o][...] = r.astype(dt)
            o += 1
        for dt in head_dtypes:
            for bb in range(tm // rpb):
                for h in range(n_heads):
                    outs[o][bb, h] = heads[h][bb * rpb:(bb + 1) * rpb].astype(dt)
            o += 1
        for dt in headt_dtypes:
            for h in range(n_heads):
                outs[o][0, h] = heads[h].T.astype(dt)
            o += 1
        if want_kmean:
            nblk = tm // MOBA_BLOCK
            for h in range(n_heads):
                outs[o][0, 0, h] = jnp.sum(heads[h].reshape(nblk, MOBA_BLOCK, HEAD_DIM),
                                           axis=1) * (1.0 / MOBA_BLOCK)
            o += 1


def _project(x2d, w_bf, col_block, col_width, segs, tm, seq, rope_tabs=None, interleave=False):
    M, D = x2d.shape
    nt = M // tm
    n_batch = M // seq
    tpb = max(seq // tm, 1)
    rpb = min(tm, seq)
    has_rope = rope_tabs is not None
    in_specs = [pl.BlockSpec((tm, D), lambda i: (i, 0)),
                pl.BlockSpec((D, col_width), lambda i: (0, col_block))]
    args = [x2d, w_bf]
    if has_rope:
        in_specs += [pl.BlockSpec((tm, HEAD_DIM), lambda i: (i % tpb, 0))] * 2
        args += list(rope_tabs)
    out_shapes, out_specs = [], []
    for c0, width, rope, row_dtypes, head_dtypes, headt_dtypes, want_kmean, scale in segs:
        n_heads = width // HEAD_DIM
        for dt in row_dtypes:
            if interleave:
                out_shapes.append(jax.ShapeDtypeStruct((M * n_heads, HEAD_DIM), dt))
                out_specs.append(pl.BlockSpec((tm * n_heads, HEAD_DIM), lambda i: (i, 0)))
            else:
                out_shapes.append(jax.ShapeDtypeStruct((M, width), dt))
                out_specs.append(pl.BlockSpec((tm, width), lambda i: (i, 0)))
        for dt in head_dtypes:
            out_shapes.append(jax.ShapeDtypeStruct((n_batch, n_heads, seq, HEAD_DIM), dt))
            out_specs.append(pl.BlockSpec((tm // rpb, n_heads, rpb, HEAD_DIM),
                                          lambda i: (i // tpb, 0, i % tpb, 0)))
        for dt in headt_dtypes:
            assert tm <= seq
            out_shapes.append(jax.ShapeDtypeStruct((n_batch, n_heads, HEAD_DIM, seq), dt))
            out_specs.append(pl.BlockSpec((1, n_heads, HEAD_DIM, tm),
                                          lambda i: (i // tpb, 0, 0, i % tpb)))
        if want_kmean:
            nblk = tm // MOBA_BLOCK
            out_shapes.append(jax.ShapeDtypeStruct((n_batch, tpb, n_heads, nblk, HEAD_DIM), F32))
            out_specs.append(pl.BlockSpec((1, 1, n_heads, nblk, HEAD_DIM),
                                          lambda i: (i // tpb, i % tpb, 0, 0, 0)))
    return pl.pallas_call(
        functools.partial(_proj_kernel, segs=segs, has_rope=has_rope, tm=tm, seq=seq,
                          interleave=interleave),
        grid=(nt,), in_specs=in_specs, out_specs=out_specs, out_shape=out_shapes,
        compiler_params=_cparams(("parallel",)), name="proj",
    )(*args)


def _sb_logits(qs, kts):
    return jnp.concatenate([_dot_nt(q, kt) for q, kt in zip(qs, kts)], axis=0)


def _sb_tile(z, vts, u2, carry_ref, acc_ref, mask):
    ck = u2.shape[1]
    nch = z.shape[1] // ck
    rows = z.shape[0]
    rows_h = rows // len(vts)

    def mask_newest(x):
        if mask is None:
            return x
        newest = jnp.where(mask, x[:, (nch - 1) * ck:], 0.0)
        return jnp.concatenate([x[:, :(nch - 1) * ck], newest], axis=1) if nch > 1 else newest

    sp = mask_newest(jnp.maximum(z, 0.0) + jnp.log(1.0 + jnp.exp(-jnp.abs(z))))
    st = (jnp.concatenate([sp[:, c * ck:(c + 1) * ck] for c in range(nch)], axis=0)
          if nch > 1 else sp)
    tail = jnp.dot(st.astype(BF16), u2, preferred_element_type=F32)
    carry = carry_ref[...]
    reps = ck // LANES
    a_parts = [None] * nch
    for c in reversed(range(nch)):
        tail_c = tail[c * rows:(c + 1) * rows]
        a_parts[c] = jnp.exp(z[:, c * ck:(c + 1) * ck] - tail_c - jnp.tile(carry, (1, reps)))
        carry = carry + tail_c[:, :1]
    a = mask_newest(jnp.concatenate(a_parts, axis=1) if nch > 1 else a_parts[0]).astype(BF16)
    for g, vt in enumerate(vts):
        sl = slice(g * rows_h, (g + 1) * rows_h)
        acc_ref[sl, :] += jnp.dot(a[sl], vt, preferred_element_type=F32)
    carry_ref[...] = carry


def _sb_prompt_kernel(q_ref, k_ref, v_ref, u_ref, o_ref, carry_ref, acc_ref, *, tq, n_grp):
    i = pl.program_id(2)
    u2 = u_ref[...]
    carry_ref[...] = jnp.zeros_like(carry_ref)
    acc_ref[...] = jnp.zeros_like(acc_ref)
    row = lax.broadcasted_iota(jnp.int32, (n_grp * tq, tq), 0)
    col = lax.broadcasted_iota(jnp.int32, (n_grp * tq, tq), 1)
    qs = [q_ref[0, g] for g in range(n_grp)]

    def logits(tile_idx):
        k0 = pl.multiple_of(tile_idx * tq, tq)
        return _sb_logits(qs, [k_ref[0, g, pl.ds(k0, tq), :] for g in range(n_grp)])

    def values(tile_idx):
        k0 = pl.multiple_of(tile_idx * tq, tq)
        return [v_ref[0, g, pl.ds(k0, tq), :] for g in range(n_grp)]

    _sb_tile(logits(i), values(i), u2, carry_ref, acc_ref, col < row % tq)

    def body(s, c):
        j = i - 1 - s
        _sb_tile(logits(j), values(j), u2, carry_ref, acc_ref, None)
        return c

    lax.fori_loop(0, i, body, 0)
    for g in range(n_grp):
        o_ref[0, g] = acc_ref[g * tq:(g + 1) * tq, :]


def _suffix_matrix(n):
    u = (np.arange(n)[:, None] >= np.arange(n)[None, :]).astype(np.float32)
    return jnp.asarray(u, dtype=BF16)


def _sb_prompt(q_bf, k_bf, v_bf, tq=256, n_grp=6):
    B, H, T, _ = q_bf.shape
    assert H % n_grp == 0
    q_spec = pl.BlockSpec((1, n_grp, tq, HEAD_DIM), lambda b, h, i: (b, h, i, 0))
    kv_spec = pl.BlockSpec((1, n_grp, T, HEAD_DIM), lambda b, h, i: (b, h, 0, 0))
    return pl.pallas_call(
        functools.partial(_sb_prompt_kernel, tq=tq, n_grp=n_grp),
        grid=(B, H // n_grp, T // tq),
        in_specs=[q_spec, kv_spec, kv_spec, pl.BlockSpec((tq, tq), lambda b, h, i: (0, 0))],
        out_specs=q_spec,
        out_shape=jax.ShapeDtypeStruct((B, H, T, HEAD_DIM), F32),
        scratch_shapes=[pltpu.VMEM((n_grp * tq, LANES), F32),
                        pltpu.VMEM((n_grp * tq, HEAD_DIM), F32)],
        compiler_params=_cparams(("parallel", "parallel", "arbitrary")), name="sb_prompt",
    )(q_bf, k_bf, v_bf, _suffix_matrix(tq))


def _select_topk(gate, past):
    lane = lax.broadcasted_iota(jnp.int32, gate.shape, 1)
    g = jnp.where(past, gate, -jnp.inf)
    sel = jnp.zeros(gate.shape, F32)
    for _ in range(MOBA_TOPK):
        m = jnp.max(g, axis=1, keepdims=True)
        idx = jnp.min(jnp.where(g == m, lane, LANES), axis=1, keepdims=True)
        hit = lane == idx
        sel = jnp.where(hit & (m > -jnp.inf), 1.0, sel)
        g = jnp.where(hit, -jnp.inf, g)
    return sel


def _gate_nt(km, qf):
    kh = km.astype(BF16)
    kl = (km - kh.astype(F32)).astype(BF16)
    qh = qf.astype(BF16)
    ql = (qf - qh.astype(F32)).astype(BF16)
    nb = km.shape[0]
    g1 = _dot_nt(jnp.concatenate([kh, kl], axis=0), qh)
    return g1[:nb] + g1[nb:] + _dot_nt(kh, ql)


def _rank_select(gate, n_past):
    nb = gate.shape[0]
    blk = lax.broadcasted_iota(jnp.int32, gate.shape, 0)
    rank = jnp.zeros(gate.shape, F32)
    for m in range(nb):
        gm = gate[m:m + 1, :]
        beats = (gm > gate) | ((gm == gate) & (blk > m))
        rank = rank + jnp.where(beats & (n_past > m), 1.0, 0.0)
    return jnp.where((blk < n_past) & (rank < MOBA_TOPK), 1.0, 0.0)


def _moba_prompt_kernel(q_ref, k_ref, vt_ref, km_ref, o_ref, *, tq, nb, n_heads):
    i = pl.program_id(1)
    hs = range(n_heads)
    qf = [q_ref[0, g] for g in hs]
    q = [(x * ATTN_SCALE).astype(BF16) for x in qf]
    gate = jnp.concatenate([_gate_nt(km_ref[0, g], qf[g]) for g in hs], axis=1)
    sel = _rank_select(gate, i)
    hcols = lambda g: slice(g * tq, (g + 1) * tq)
    key = lax.broadcasted_iota(jnp.int32, (tq, n_heads * tq), 0)
    qry = lax.broadcasted_iota(jnp.int32, (tq, n_heads * tq), 1) % tq

    def attend(nq):
        nk = (nq + 1) * tq
        s = jnp.concatenate([_dot_nt(k_ref[0, g, :nk, :], q[g]) for g in hs], axis=1)
        blocks = [jnp.where(sel[n:n + 1, :] > 0.0, s[n * tq:(n + 1) * tq], NEG_BIG)
                  for n in range(nq)]
        blocks.append(jnp.where(key <= qry, s[nq * tq:], NEG_BIG))
        s = jnp.concatenate(blocks, axis=0) if nq else blocks[0]
        m = jnp.max(s, axis=0, keepdims=True)
        p = jnp.exp(s - m)
        l = jnp.sum(p, axis=0, keepdims=True)
        p = p.astype(BF16)
        o = jnp.concatenate(
            [jnp.dot(vt_ref[0, g, :, :nk], p[:, hcols(g)], preferred_element_type=F32)
             for g in hs], axis=1) / l
        for g in hs:
            o_ref[0, g] = o[:, hcols(g)].T

    for nq in range(nb):
        pl.when(i == nq)(functools.partial(attend, nq))


def _moba_prompt(q_f32, k_bf, vt_bf, kmean):
    B, H, T, _ = q_f32.shape
    tq = MOBA_BLOCK
    nb = T // tq
    q_spec = pl.BlockSpec((1, H, tq, HEAD_DIM), lambda b, i: (b, 0, i, 0))
    return pl.pallas_call(
        functools.partial(_moba_prompt_kernel, tq=tq, nb=nb, n_heads=H),
        grid=(B, nb),
        in_specs=[q_spec,
                  pl.BlockSpec((1, H, T, HEAD_DIM), lambda b, i: (b, 0, 0, 0)),
                  pl.BlockSpec((1, H, HEAD_DIM, T), lambda b, i: (b, 0, 0, 0)),
                  pl.BlockSpec((1, H, nb, HEAD_DIM), lambda b, i: (b, 0, 0, 0))],
        out_specs=q_spec,
        out_shape=jax.ShapeDtypeStruct((B, H, T, HEAD_DIM), F32),
        compiler_params=_cparams(("parallel", "arbitrary")), name="moba_prompt",
    )(q_f32, k_bf, vt_bf, kmean)


def _mem_attn_kernel(q_ref, k_ref, v_ref, o_ref, *, n_heads):
    n_mem = k_ref.shape[1] // n_heads
    for h in range(n_heads):
        rows = pl.ds(h, n_mem, stride=n_heads)
        s = _dot_nt((q_ref[0, h] * ATTN_SCALE).astype(BF16), k_ref[0, rows, :].astype(BF16))
        m = jnp.max(s, axis=1, keepdims=True)
        p = jnp.exp(s - m)
        l = jnp.sum(p, axis=1, keepdims=True)
        o = jnp.dot(p.astype(BF16), v_ref[0, rows, :].astype(BF16), preferred_element_type=F32)
        o_ref[0, h] = o / l


def _mem_attn(q, mk, mv, tq):
    B, H, T, _ = q.shape
    n_mem = mk.shape[1] // H
    q_spec = pl.BlockSpec((1, H, tq, HEAD_DIM), lambda b, i: (b, 0, i, 0))
    kv_spec = pl.BlockSpec((1, n_mem * H, HEAD_DIM), lambda b, i: (b, 0, 0))
    return pl.pallas_call(
        functools.partial(_mem_attn_kernel, n_heads=H),
        grid=(B, T // tq),
        in_specs=[q_spec, kv_spec, kv_spec],
        out_specs=q_spec,
        out_shape=jax.ShapeDtypeStruct((B, H, T, HEAD_DIM), F32),
        compiler_params=_cparams(("parallel", "arbitrary")), name="mem_attn",
    )(q, mk, mv)


def _merge_kernel(x_ref, oa_ref, ob_ref, om_ref, ga_ref, gb_ref, gm_ref,
                  na_ref, nb_ref, nm_ref, w_ref, lg_ref, lb_ref, y_ref, *, alpha):
    def group(o_ref, g_ref, n_ref):
        o = jnp.concatenate([_heads_to_lanes(o_ref, (bb,)) for bb in range(o_ref.shape[0])],
                            axis=0)
        r = o * lax.rsqrt(jnp.mean(o * o, axis=-1, keepdims=True) + RMS_EPS) * n_ref[...]
        g = g_ref[...]
        return (r * (g * (1.0 / (1.0 + jnp.exp(-g))))).astype(BF16)

    mix = jnp.concatenate([group(oa_ref, ga_ref, na_ref), group(ob_ref, gb_ref, nb_ref),
                           group(om_ref, gm_ref, nm_ref)], axis=1)
    sub = jnp.dot(mix, w_ref[...], preferred_element_type=F32)
    h = alpha * x_ref[...] + sub
    mu = jnp.mean(h, axis=-1, keepdims=True)
    d = h - mu
    var = jnp.mean(d * d, axis=-1, keepdims=True)
    y_ref[...] = d * lax.rsqrt(var + LN_EPS) * lg_ref[...] + lb_ref[...]


def _merge(x2d, o_a, o_b, o_m, g_a, g_b, g_m, norm_a, norm_b, norm_m, w_out_bf, ln_g, ln_b,
           alpha, tm):
    M, D = x2d.shape
    seq = o_a.shape[2]
    tpb = max(seq // tm, 1)
    rpb = min(tm, seq)
    row = lambda a: pl.BlockSpec((tm, a.shape[1]), lambda i: (i, 0))
    head = lambda a: pl.BlockSpec((tm // rpb, a.shape[1], rpb, HEAD_DIM),
                                  lambda i: (i // tpb, 0, i % tpb, 0))
    full = lambda a: pl.BlockSpec(a.shape, lambda i: (0, 0))
    vecs = [v.reshape(1, -1) for v in (norm_a, norm_b, norm_m, ln_g, ln_b)]
    args = [x2d, o_a, o_b, o_m, g_a, g_b, g_m, *vecs[:3], w_out_bf, *vecs[3:]]
    in_specs = ([row(x2d)] + [head(a) for a in (o_a, o_b, o_m)] + [row(a) for a in (g_a, g_b, g_m)]
                + [full(a) for a in args[7:]])
    return pl.pallas_call(
        functools.partial(_merge_kernel, alpha=alpha),
        grid=(M // tm,), in_specs=in_specs,
        out_specs=pl.BlockSpec((tm, D), lambda i: (i, 0)),
        out_shape=jax.ShapeDtypeStruct((M, D), F32),
        compiler_params=_cparams(("parallel",)), name="merge",
    )(*args)


def _block_diag_queries(q_ref):
    _, H, Tq, d = q_ref.shape
    zero = jnp.zeros((Tq, d), q_ref.dtype)
    return jnp.concatenate(
        [jnp.concatenate([q_ref[0, h] if hh == h else zero for hh in range(H)], axis=1)
         for h in range(H)], axis=0)


def _new_token_rows(t_ref):
    rows = _heads_to_lanes(t_ref, (0,))
    pad = jnp.zeros((LANES - rows.shape[0], rows.shape[1]), rows.dtype)
    return jnp.concatenate([rows, pad], axis=0).astype(BF16)


def _page_rows(page_refs, dtype):
    return jnp.concatenate([_heads_to_lanes(r).astype(dtype) for r in page_refs], axis=0)


def _take_block_diag(acc, o_ref, n_heads, tq):
    for h in range(n_heads):
        o_ref[0, h] = acc[h * tq:(h + 1) * tq, h * HEAD_DIM:(h + 1) * HEAD_DIM]


def _sb_sample_kernel(pt_ref, q_ref, *rest, n_heads, tq, n_steps, n_pg):
    k_refs, v_refs = rest[:n_pg], rest[n_pg:2 * n_pg]
    kn_ref, vn_ref, u_ref, un_ref, o_ref, carry_ref, acc_ref = rest[2 * n_pg:]
    p = pl.program_id(1)
    q = _block_diag_queries(q_ref).astype(BF16)
    rows = n_heads * tq

    @pl.when(p == 0)
    def _():
        carry_ref[...] = jnp.zeros_like(carry_ref)
        acc_ref[...] = jnp.zeros_like(acc_ref)
        row = lax.broadcasted_iota(jnp.int32, (rows, LANES), 0)
        col = lax.broadcasted_iota(jnp.int32, (rows, LANES), 1)
        _sb_tile(_sb_logits([q], [_new_token_rows(kn_ref)]), [_new_token_rows(vn_ref)],
                 un_ref[...], carry_ref, acc_ref, col < row % tq)

    _sb_tile(_sb_logits([q], [_page_rows(k_refs, BF16)]), [_page_rows(v_refs, BF16)],
             u_ref[...], carry_ref, acc_ref, None)

    @pl.when(p == n_steps - 1)
    def _():
        _take_block_diag(acc_ref[...], o_ref, n_heads, tq)


def _sb_sample(q, k_pool, v_pool, k_new, v_new, page_table):
    B, H, tq, _ = q.shape
    rows, W = H * tq, H * HEAD_DIM
    page = k_pool.shape[2]
    n_pg = SAMPLE_PAGES_PER_STEP
    n_steps = page_table.shape[1] // n_pg
    assert n_steps * n_pg == page_table.shape[1]

    def pg(j):
        return pl.BlockSpec((None, H, page, HEAD_DIM),
                            lambda b, p, pt: (pt[b, n_pg * (n_steps - 1 - p) + j], 0, 0, 0))

    per_b = pl.BlockSpec((1, H, tq, HEAD_DIM), lambda b, p, pt: (b, 0, 0, 0))
    u2, un2 = _suffix_matrix(2 * page), _suffix_matrix(LANES)
    const = lambda a: pl.BlockSpec(a.shape, lambda b, p, pt: (0, 0))
    pages = [pg(j) for j in range(n_pg)]
    grid_spec = pltpu.PrefetchScalarGridSpec(
        num_scalar_prefetch=1, grid=(B, n_steps),
        in_specs=[per_b, *pages, *pages, per_b, per_b, const(u2), const(un2)],
        out_specs=per_b,
        scratch_shapes=[pltpu.VMEM((rows, LANES), F32), pltpu.VMEM((rows, W), F32)])
    return pl.pallas_call(
        functools.partial(_sb_sample_kernel, n_heads=H, tq=tq, n_steps=n_steps, n_pg=n_pg),
        grid_spec=grid_spec, out_shape=jax.ShapeDtypeStruct((B, H, tq, HEAD_DIM), F32),
        compiler_params=_cparams(("parallel", "arbitrary")), name="sb_sample",
    )(page_table, q, *([k_pool] * n_pg), *([v_pool] * n_pg), k_new, v_new, u2, un2)


def _moba_sample_kernel(pt_ref, qf_ref, *rest, n_heads, tq, n_steps, n_pg):
    k_refs, v_refs = rest[:n_pg], rest[n_pg:2 * n_pg]
    kn_ref, vn_ref, o_ref, km_ref, m_ref, l_ref, ob_ref = rest[2 * n_pg:]
    n = pl.program_id(1)
    rows = n_heads * tq
    blk = MOBA_BLOCK
    bps = n_pg // 2
    nb = n_steps * bps
    qf = _block_diag_queries(qf_ref)
    q = (qf * ATTN_SCALE).astype(BF16)
    lane = lax.broadcasted_iota(jnp.int32, (rows, LANES), 1)
    diag = lambda o: jnp.concatenate(
        [o[h * tq:(h + 1) * tq, h * HEAD_DIM:(h + 1) * HEAD_DIM] for h in range(n_heads)], axis=0)

    @pl.when(n == 0)
    def _():
        km_ref[...] = jnp.zeros_like(km_ref)
        m_ref[...] = jnp.zeros_like(m_ref)
        l_ref[...] = jnp.zeros_like(l_ref)

    pages = [_heads_to_lanes(r) for r in k_refs]
    kc = jnp.concatenate([pg.astype(BF16) for pg in pages], axis=0)
    s = _dot_nt(q, kc)
    m_all, l_all = m_ref[...], l_ref[...]
    for c in range(bps):
        j = n * bps + c
        km_ref[pl.ds(j, 1), :] = (jnp.sum(pages[2 * c], axis=0, keepdims=True)
                                  + jnp.sum(pages[2 * c + 1], axis=0, keepdims=True)) * (1.0 / blk)
        sc = s[:, c * blk:(c + 1) * blk]
        mc = jnp.max(sc, axis=1, keepdims=True)
        pc = jnp.exp(sc - mc)
        ob_ref[j] = diag(jnp.dot(pc.astype(BF16), _page_rows(v_refs[2 * c:2 * c + 2], BF16),
                                 preferred_element_type=F32))
        m_all = jnp.where(lane == j, mc, m_all)
        l_all = jnp.where(lane == j, jnp.sum(pc, axis=1, keepdims=True), l_all)
    m_ref[...] = m_all
    l_ref[...] = l_all

    @pl.when(n == n_steps - 1)
    def _():
        gate = _dot_nt(qf, km_ref[...], precision=lax.Precision.HIGHEST)
        sel = _select_topk(gate, lane < nb) > 0.0
        row = lax.broadcasted_iota(jnp.int32, (rows, LANES), 0)
        s_new = jnp.where(lane <= row % tq, _dot_nt(q, _new_token_rows(kn_ref)), NEG_BIG)
        m_blk = m_ref[...]
        m = jnp.maximum(jnp.max(s_new, axis=1, keepdims=True),
                        jnp.max(jnp.where(sel, m_blk, NEG_BIG), axis=1, keepdims=True))
        p_new = jnp.exp(s_new - m)
        w = jnp.where(sel, jnp.exp(jnp.where(sel, m_blk, NEG_BIG) - m), 0.0)
        l = (jnp.sum(p_new, axis=1, keepdims=True)
             + jnp.sum(w * l_ref[...], axis=1, keepdims=True))
        acc = diag(jnp.dot(p_new.astype(BF16), _new_token_rows(vn_ref),
                           preferred_element_type=F32))
        for j in range(nb):
            acc = acc + w[:, j:j + 1] * ob_ref[j]
        o = acc / l
        for h in range(n_heads):
            o_ref[0, h] = o[h * tq:(h + 1) * tq]


def _moba_sample(q, k_pool, v_pool, k_new, v_new, page_table):
    B, H, tq, _ = q.shape
    rows, W = H * tq, H * HEAD_DIM
    page = k_pool.shape[2]
    assert 2 * page == MOBA_BLOCK
    n_pg = SAMPLE_PAGES_PER_STEP
    n_steps = page_table.shape[1] // n_pg
    assert n_steps * n_pg == page_table.shape[1] and n_pg % 2 == 0
    nb = page_table.shape[1] // 2
    assert nb <= LANES

    def pg(j):
        return pl.BlockSpec((None, H, page, HEAD_DIM),
                            lambda b, n, pt: (pt[b, n_pg * n + j], 0, 0, 0))

    per_b = pl.BlockSpec((1, H, tq, HEAD_DIM), lambda b, n, pt: (b, 0, 0, 0))
    pages = [pg(j) for j in range(n_pg)]
    grid_spec = pltpu.PrefetchScalarGridSpec(
        num_scalar_prefetch=1, grid=(B, n_steps),
        in_specs=[per_b, *pages, *pages, per_b, per_b],
        out_specs=per_b,
        scratch_shapes=[pltpu.VMEM((LANES, W), F32),
                        pltpu.VMEM((rows, LANES), F32),
                        pltpu.VMEM((rows, LANES), F32),
                        pltpu.VMEM((nb, rows, HEAD_DIM), F32)])
    return pl.pallas_call(
        functools.partial(_moba_sample_kernel, n_heads=H, tq=tq, n_steps=n_steps, n_pg=n_pg),
        grid_spec=grid_spec, out_shape=jax.ShapeDtypeStruct((B, H, tq, HEAD_DIM), F32),
        compiler_params=_cparams(("parallel", "arbitrary")), name="moba_sample",
    )(page_table, q, *([k_pool] * n_pg), *([v_pool] * n_pg), k_new, v_new)


def _rope_tables(pos):
    inv = ROPE_THETA ** (-jnp.arange(0, HEAD_DIM, 2, dtype=F32) / HEAD_DIM)
    ang = pos.astype(F32)[:, None] * inv[None, :]
    cos, sin = jnp.cos(ang), jnp.sin(ang)
    return jnp.concatenate([cos, cos], -1), jnp.concatenate([-sin, sin], -1)


def _project_all(x2d, w_in_bf, w_sb, w_moba, w_mem, tm, seq, rope_tabs, prompt):
    assert w_sb == w_moba and (8 * w_sb) % (2 * w_mem) == 0
    grp = 4 * w_sb
    kv = (F32, BF16) if prompt else (F32,)
    seg = lambda k, width, rope=False, rows=(), heads=(), headt=(), km=False, scale=None: (
        k * width, width, rope, rows, heads, headt, km, scale)
    a = _project(x2d, w_in_bf, 0, grp,
                 (seg(0, w_sb, heads=(BF16,) if prompt else (F32,), scale=ATTN_SCALE),
                  seg(1, w_sb, heads=kv),
                  seg(2, w_sb, heads=kv), seg(3, w_sb, rows=(F32,))), tm, seq)
    b = _project(x2d, w_in_bf, 1, grp,
                 (seg(0, w_moba, rope=True, heads=(F32,)),
                  seg(1, w_moba, rope=True, heads=kv, km=prompt),
                  seg(2, w_moba, heads=(F32,), headt=(BF16,) if prompt else ()),
                  seg(3, w_moba, rows=(F32,))), tm, seq, rope_tabs=rope_tabs)
    qm, gm = _project(x2d, w_in_bf, (2 * grp) // (2 * w_mem), 2 * w_mem,
                      (seg(0, w_mem, heads=(F32,)), seg(1, w_mem, rows=(F32,))), tm, seq)
    if prompt:
        names_a = ("qa", "ka", "ka_bf", "va", "va_bf", "ga")
        names_b = ("qb", "kb", "kb_bf", "kmean", "vb", "vbt_bf", "gb")
    else:
        names_a = ("qa", "ka", "va", "ga")
        names_b = ("qb", "kb", "vb", "gb")
    out = dict(zip(names_a, a))
    out.update(zip(names_b, b))
    out.update(qm=qm, gm=gm)
    return out


def kernel(x_prompt, x_sample, cache_sb_k, cache_sb_v, cache_moba_k, cache_moba_v,
           cache_mem_k, cache_mem_v, page_table, mem_prompt,
           w_in, w_mem_k, w_mem_v, norm_a, norm_b, norm_m, w_out, ln_g, ln_b):
    depth = w_in.shape[0]
    B, T, D = x_prompt.shape
    Bs, Ts, _ = x_sample.shape
    h_sb, h_moba, h_mem = cache_sb_k.shape[3], cache_moba_k.shape[3], cache_mem_k.shape[3]
    w_sb, w_moba, w_mem = h_sb * HEAD_DIM, h_moba * HEAD_DIM, h_mem * HEAD_DIM
    n_mem = mem_prompt.shape[1]
    page = cache_sb_k.shape[2]
    past_len = page_table.shape[1] * page
    assert past_len % MOBA_BLOCK == 0 and Ts <= LANES and T % MOBA_BLOCK == 0
    alpha = (2.0 * depth) ** 0.25

    tm = 512
    rope_p = _rope_tables(jnp.arange(T, dtype=jnp.int32))
    rope_s = tuple(jnp.tile(t, (Bs, 1)) for t in
                   _rope_tables(past_len + jnp.arange(Ts, dtype=jnp.int32)))

    y_p = x_prompt.reshape(B * T, D)
    y_s = x_sample.reshape(Bs * Ts, D)
    outs = [[] for _ in range(10)]
    for l in range(depth):
        w_in_bf = w_in[l].astype(BF16)
        w_out_bf = w_out[l].astype(BF16)
        w_memkv_bf = jnp.concatenate([w_mem_k[l], w_mem_v[l]], axis=1).astype(BF16)

        pp = _project_all(y_p, w_in_bf, w_sb, w_moba, w_mem, tm, T, rope_p, True)
        o_a = _sb_prompt(pp["qa"], pp["ka_bf"], pp["va_bf"])
        kmean = jnp.transpose(pp["kmean"], (0, 2, 1, 3, 4)).reshape(
            B, h_moba, T // MOBA_BLOCK, HEAD_DIM)
        o_b = _moba_prompt(pp["qb"], pp["kb_bf"], pp["vbt_bf"], kmean)
        mk, mv = _project(
            mem_prompt.reshape(B * n_mem, D), w_memkv_bf, 0, 2 * w_mem,
            ((0, w_mem, False, (F32,), (), (), False, None),
             (w_mem, w_mem, False, (F32,), (), (), False, None)),
            256, n_mem, interleave=True)
        tok_head = lambda a: a.reshape(a.shape[0], n_mem * h_mem, HEAD_DIM)
        o_m = _mem_attn(pp["qm"], mk.reshape(B, n_mem * h_mem, HEAD_DIM),
                        mv.reshape(B, n_mem * h_mem, HEAD_DIM), 1024)
        y_p_new = _merge(y_p, o_a, o_b, o_m, pp["ga"], pp["gb"], pp["gm"],
                         norm_a[l], norm_b[l], norm_m[l], w_out_bf, ln_g[l], ln_b[l], alpha, 512)

        ps = _project_all(y_s, w_in_bf, w_sb, w_moba, w_mem, Bs * Ts, Ts, rope_s, False)
        pool = lambda c: jnp.transpose(c[l], (0, 2, 1, 3))
        so_a = _sb_sample(ps["qa"], pool(cache_sb_k), pool(cache_sb_v), ps["ka"], ps["va"],
                          page_table)
        so_b = _moba_sample(ps["qb"], pool(cache_moba_k), pool(cache_moba_v), ps["kb"],
                            ps["vb"], page_table)
        so_m = _mem_attn(ps["qm"], tok_head(cache_mem_k[l]), tok_head(cache_mem_v[l]), Ts)
        y_s_new = _merge(y_s, so_a, so_b, so_m, ps["ga"], ps["gb"], ps["gm"],
                         norm_a[l], norm_b[l], norm_m[l], w_out_bf, ln_g[l], ln_b[l], alpha,
                         Bs * Ts)

        tok_major = lambda a: jnp.transpose(a, (0, 2, 1, 3))
        new = [tok_major(pp["ka"]), tok_major(pp["va"]), tok_major(pp["kb"]), tok_major(pp["vb"]),
               mk.reshape(B, n_mem, h_mem, HEAD_DIM), mv.reshape(B, n_mem, h_mem, HEAD_DIM),
               tok_major(ps["ka"]), tok_major(ps["va"]), tok_major(ps["kb"]), tok_major(ps["vb"])]
        for lst, a in zip(outs, new):
            lst.append(a)
        y_p, y_s = y_p_new, y_s_new

    return (y_p.reshape(B, T, D), y_s.reshape(Bs, Ts, D), *[jnp.stack(o) for o in outs])
```

```python
import functools

import jax
import jax.numpy as jnp
import numpy as np
from jax import lax
from jax.experimental import pallas as pl
from jax.experimental.pallas import tpu as pltpu

F32 = jnp.float32
BF16 = jnp.bfloat16

HEAD_DIM = 128
MOBA_BLOCK = 256
MOBA_TOPK = 3
ROPE_THETA = 10000.0
LN_EPS = 1e-5
RMS_EPS = 1e-6
ATTN_SCALE = HEAD_DIM ** -0.5
NEG_BIG = -1e30
LANES = 128
VMEM_LIMIT = 56 * 1024 * 1024
SAMPLE_PAGES_PER_STEP = 16


def _cparams(sem):
    return pltpu.CompilerParams(dimension_semantics=sem, vmem_limit_bytes=VMEM_LIMIT)


def _dot_nt(a, b, precision=None):
    return lax.dot_general(a, b, (((1,), (1,)), ((), ())), precision=precision,
                           preferred_element_type=F32)


def _heads_to_lanes(ref, idx=()):
    n_heads = ref.shape[len(idx)]
    return jnp.concatenate([ref[idx + (h,)] for h in range(n_heads)], axis=1)


def _proj_kernel(*refs, segs, has_rope, tm, seq, interleave, seg_steps):
    x_ref, w_ref = refs[0], refs[1]
    pos = 2
    if has_rope:
        cos_ref, sin_ref = refs[2], refs[3]
        pos = 4
    all_outs = refs[pos:]
    rpb = min(tm, seq)
    x_all = None if seg_steps else x_ref[...].astype(BF16)

    def segment(seg, outs):
        c0, width, rope, row_dtypes, head_dtypes, headt_dtypes, want_kmean, scale = seg
        o = 0
        x = x_ref[...].astype(BF16) if seg_steps else x_all
        w = w_ref[...] if seg_steps else w_ref[:, c0:c0 + width]
        r = jnp.dot(x, w, preferred_element_type=F32)
        n_heads = width // HEAD_DIM
        if scale is not None:
            r = r * scale
        heads = [r[:, h * HEAD_DIM:(h + 1) * HEAD_DIM] for h in range(n_heads)]
        if rope:
            cos = cos_ref[...]
            sin = sin_ref[...]
            heads = [xh * cos + pltpu.roll(xh, HEAD_DIM // 2, axis=1) * sin for xh in heads]
            r = jnp.concatenate(heads, axis=1)
        for dt in row_dtypes:
            if interleave:
                for h in range(n_heads):
                    outs[o][pl.ds(h, tm, stride=n_heads), :] = heads[h].astype(dt)
            else:
                outs[o][...] = r.astype(dt)
            o += 1
        for dt in head_dtypes:
            for bb in range(tm // rpb):
                for h in range(n_heads):
                    outs[o][bb, h] = heads[h][bb * rpb:(bb + 1) * rpb].astype(dt)
            o += 1
        for dt in headt_dtypes:
            for h in range(n_heads):
                outs[o][0, h] = heads[h].T.astype(dt)
            o += 1
        if want_kmean:
            nblk = tm // MOBA_BLOCK
            for h in range(n_heads):
                outs[o][0, 0, h] = jnp.sum(heads[h].reshape(nblk, MOBA_BLOCK, HEAD_DIM),
                                           axis=1) * (1.0 / MOBA_BLOCK)

    first = 0
    for idx, seg in enumerate(segs):
        n_out = len(seg[3]) + len(seg[4]) + len(seg[5]) + int(seg[6])
        outs = all_outs[first:first + n_out]
        first += n_out
        if seg_steps:
            pl.when(pl.program_id(0) == idx)(functools.partial(segment, seg, outs))
        else:
            segment(seg, outs)


def _project(x2d, w_bf, col_block, col_width, segs, tm, seq, rope_tabs=None, interleave=False,
             seg_steps=False):
    M, D = x2d.shape
    nt = M // tm
    n_batch = M // seq
    tpb = max(seq // tm, 1)
    rpb = min(tm, seq)
    has_rope = rope_tabs is not None
    if seg_steps:
        seg_w = segs[0][1]
        assert nt == 1 and all(sg[1] == seg_w and sg[0] == k * seg_w for k, sg in enumerate(segs))
        tile = lambda i: 0
        w_spec = pl.BlockSpec((D, seg_w), lambda i: (0, col_block * (col_width // seg_w) + i))
    else:
        tile = lambda i: i
        w_spec = pl.BlockSpec((D, col_width), lambda i: (0, col_block))
    in_specs = [pl.BlockSpec((tm, D), lambda i: (tile(i), 0)), w_spec]
    args = [x2d, w_bf]
    if has_rope:
        in_specs += [pl.BlockSpec((tm, HEAD_DIM), lambda i: (tile(i) % tpb, 0))] * 2
        args += list(rope_tabs)
    out_shapes, out_specs = [], []
    for c0, width, rope, row_dtypes, head_dtypes, headt_dtypes, want_kmean, scale in segs:
        n_heads = width // HEAD_DIM
        for dt in row_dtypes:
            if interleave:
                out_shapes.append(jax.ShapeDtypeStruct((M * n_heads, HEAD_DIM), dt))
                out_specs.append(pl.BlockSpec((tm * n_heads, HEAD_DIM), lambda i: (tile(i), 0)))
            else:
                out_shapes.append(jax.ShapeDtypeStruct((M, width), dt))
                out_specs.append(pl.BlockSpec((tm, width), lambda i: (tile(i), 0)))
        for dt in head_dtypes:
            out_shapes.append(jax.ShapeDtypeStruct((n_batch, n_heads, seq, HEAD_DIM), dt))
            out_specs.append(pl.BlockSpec((tm // rpb, n_heads, rpb, HEAD_DIM),
                                          lambda i: (tile(i) // tpb, 0, tile(i) % tpb, 0)))
        for dt in headt_dtypes:
            assert tm <= seq
            out_shapes.append(jax.ShapeDtypeStruct((n_batch, n_heads, HEAD_DIM, seq), dt))
            out_specs.append(pl.BlockSpec((1, n_heads, HEAD_DIM, tm),
                                          lambda i: (tile(i) // tpb, 0, 0, tile(i) % tpb)))
        if want_kmean:
            nblk = tm // MOBA_BLOCK
            out_shapes.append(jax.ShapeDtypeStruct((n_batch, tpb, n_heads, nblk, HEAD_DIM), F32))
            out_specs.append(pl.BlockSpec((1, 1, n_heads, nblk, HEAD_DIM),
                                          lambda i: (tile(i) // tpb, tile(i) % tpb, 0, 0, 0)))
    return pl.pallas_call(
        functools.partial(_proj_kernel, segs=segs, has_rope=has_rope, tm=tm, seq=seq,
                          interleave=interleave, seg_steps=seg_steps),
        grid=(len(segs) if seg_steps else nt,), in_specs=in_specs, out_specs=out_specs,
        out_shape=out_shapes,
        compiler_params=_cparams(("arbitrary" if seg_steps else "parallel",)), name="proj",
    )(*args)


def _sb_logits(qs, kts):
    return jnp.concatenate([_dot_nt(q, kt) for q, kt in zip(qs, kts)], axis=0)


def _sb_tile(z, vts, u2, carry_ref, acc_ref, mask):
    ck = u2.shape[1]
    nch = z.shape[1] // ck
    rows = z.shape[0]
    rows_h = rows // len(vts)

    def mask_newest(x):
        if mask is None:
            return x
        newest = jnp.where(mask, x[:, (nch - 1) * ck:], 0.0)
        return jnp.concatenate([x[:, :(nch - 1) * ck], newest], axis=1) if nch > 1 else newest

    sp = mask_newest(jnp.maximum(z, 0.0) + jnp.log(1.0 + jnp.exp(-jnp.abs(z))))
    st = (jnp.concatenate([sp[:, c * ck:(c + 1) * ck] for c in range(nch)], axis=0)
          if nch > 1 else sp)
    tail = jnp.dot(st.astype(BF16), u2, preferred_element_type=F32)
    carry = carry_ref[...]
    reps = ck // LANES
    a_parts = [None] * nch
    for c in reversed(range(nch)):
        tail_c = tail[c * rows:(c + 1) * rows]
        a_parts[c] = jnp.exp(z[:, c * ck:(c + 1) * ck] - tail_c - jnp.tile(carry, (1, reps)))
        carry = carry + tail_c[:, :1]
    a = mask_newest(jnp.concatenate(a_parts, axis=1) if nch > 1 else a_parts[0]).astype(BF16)
    for g, vt in enumerate(vts):
        sl = slice(g * rows_h, (g + 1) * rows_h)
        acc_ref[sl, :] += jnp.dot(a[sl], vt, preferred_element_type=F32)
    carry_ref[...] = carry


def _sb_prompt_kernel(q_ref, k_ref, v_ref, u_ref, o_ref, carry_ref, acc_ref, *, tq, n_grp):
    i = pl.program_id(2)
    u2 = u_ref[...]
    carry_ref[...] = jnp.zeros_like(carry_ref)
    acc_ref[...] = jnp.zeros_like(acc_ref)
    row = lax.broadcasted_iota(jnp.int32, (n_grp * tq, tq), 0)
    col = lax.broadcasted_iota(jnp.int32, (n_grp * tq, tq), 1)
    qs = [q_ref[0, g] for g in range(n_grp)]

    def logits(tile_idx):
        k0 = pl.multiple_of(tile_idx * tq, tq)
        return _sb_logits(qs, [k_ref[0, g, pl.ds(k0, tq), :] for g in range(n_grp)])

    def values(tile_idx):
        k0 = pl.multiple_of(tile_idx * tq, tq)
        return [v_ref[0, g, pl.ds(k0, tq), :] for g in range(n_grp)]

    _sb_tile(logits(i), values(i), u2, carry_ref, acc_ref, col < row % tq)

    def body(s, c):
        j = i - 1 - s
        _sb_tile(logits(j), values(j), u2, carry_ref, acc_ref, None)
        return c

    lax.fori_loop(0, i, body, 0)
    for g in range(n_grp):
        o_ref[0, g] = acc_ref[g * tq:(g + 1) * tq, :]


def _suffix_matrix(n):
    u = (np.arange(n)[:, None] >= np.arange(n)[None, :]).astype(np.float32)
    return jnp.asarray(u, dtype=BF16)


def _sb_prompt(q_bf, k_bf, v_bf, tq=256, n_grp=6):
    B, H, T, _ = q_bf.shape
    assert H % n_grp == 0
    q_spec = pl.BlockSpec((1, n_grp, tq, HEAD_DIM), lambda b, h, i: (b, h, i, 0))
    kv_spec = pl.BlockSpec((1, n_grp, T, HEAD_DIM), lambda b, h, i: (b, h, 0, 0))
    return pl.pallas_call(
        functools.partial(_sb_prompt_kernel, tq=tq, n_grp=n_grp),
        grid=(B, H // n_grp, T // tq),
        in_specs=[q_spec, kv_spec, kv_spec, pl.BlockSpec((tq, tq), lambda b, h, i: (0, 0))],
        out_specs=q_spec,
        out_shape=jax.ShapeDtypeStruct((B, H, T, HEAD_DIM), F32),
        scratch_shapes=[pltpu.VMEM((n_grp * tq, LANES), F32),
                        pltpu.VMEM((n_grp * tq, HEAD_DIM), F32)],
        compiler_params=_cparams(("parallel", "parallel", "arbitrary")), name="sb_prompt",
    )(q_bf, k_bf, v_bf, _suffix_matrix(tq))


def _select_topk(gate, past):
    lane = lax.broadcasted_iota(jnp.int32, gate.shape, 1)
    g = jnp.where(past, gate, -jnp.inf)
    sel = jnp.zeros(gate.shape, F32)
    for _ in range(MOBA_TOPK):
        m = jnp.max(g, axis=1, keepdims=True)
        idx = jnp.min(jnp.where(g == m, lane, LANES), axis=1, keepdims=True)
        hit = lane == idx
        sel = jnp.where(hit & (m > -jnp.inf), 1.0, sel)
        g = jnp.where(hit, -jnp.inf, g)
    return sel


def _gate_nt(km, qf):
    kh = km.astype(BF16)
    kl = (km - kh.astype(F32)).astype(BF16)
    qh = qf.astype(BF16)
    ql = (qf - qh.astype(F32)).astype(BF16)
    nb = km.shape[0]
    g1 = _dot_nt(jnp.concatenate([kh, kl], axis=0), qh)
    return g1[:nb] + g1[nb:] + _dot_nt(kh, ql)


def _rank_select(gate, n_past):
    nb = gate.shape[0]
    blk = lax.broadcasted_iota(jnp.int32, gate.shape, 0)
    rank = jnp.zeros(gate.shape, F32)
    for m in range(nb):
        gm = gate[m:m + 1, :]
        beats = (gm > gate) | ((gm == gate) & (blk > m))
        rank = rank + jnp.where(beats & (n_past > m), 1.0, 0.0)
    return jnp.where((blk < n_past) & (rank < MOBA_TOPK), 1.0, 0.0)


def _moba_prompt_kernel(q_ref, k_ref, vt_ref, km_ref, o_ref, *, tq, nb, n_heads):
    i = pl.program_id(1)
    hs = range(n_heads)
    qf = [q_ref[0, g] for g in hs]
    q = [(x * ATTN_SCALE).astype(BF16) for x in qf]
    gate = jnp.concatenate([_gate_nt(km_ref[0, g], qf[g]) for g in hs], axis=1)
    sel = _rank_select(gate, i)
    hcols = lambda g: slice(g * tq, (g + 1) * tq)
    key = lax.broadcasted_iota(jnp.int32, (tq, n_heads * tq), 0)
    qry = lax.broadcasted_iota(jnp.int32, (tq, n_heads * tq), 1) % tq

    def attend(nq):
        nk = (nq + 1) * tq
        s = jnp.concatenate([_dot_nt(k_ref[0, g, :nk, :], q[g]) for g in hs], axis=1)
        blocks = [jnp.where(sel[n:n + 1, :] > 0.0, s[n * tq:(n + 1) * tq], NEG_BIG)
                  for n in range(nq)]
        blocks.append(jnp.where(key <= qry, s[nq * tq:], NEG_BIG))
        s = jnp.concatenate(blocks, axis=0) if nq else blocks[0]
        m = jnp.max(s, axis=0, keepdims=True)
        p = jnp.exp(s - m)
        l = jnp.sum(p, axis=0, keepdims=True)
        p = p.astype(BF16)
        o = jnp.concatenate(
            [jnp.dot(vt_ref[0, g, :, :nk], p[:, hcols(g)], preferred_element_type=F32)
             for g in hs], axis=1) / l
        for g in hs:
            o_ref[0, g] = o[:, hcols(g)].T

    for nq in range(nb):
        pl.when(i == nq)(functools.partial(attend, nq))


def _moba_prompt(q_f32, k_bf, vt_bf, kmean):
    B, H, T, _ = q_f32.shape
    tq = MOBA_BLOCK
    nb = T // tq
    q_spec = pl.BlockSpec((1, H, tq, HEAD_DIM), lambda b, i: (b, 0, i, 0))
    return pl.pallas_call(
        functools.partial(_moba_prompt_kernel, tq=tq, nb=nb, n_heads=H),
        grid=(B, nb),
        in_specs=[q_spec,
                  pl.BlockSpec((1, H, T, HEAD_DIM), lambda b, i: (b, 0, 0, 0)),
                  pl.BlockSpec((1, H, HEAD_DIM, T), lambda b, i: (b, 0, 0, 0)),
                  pl.BlockSpec((1, H, nb, HEAD_DIM), lambda b, i: (b, 0, 0, 0))],
        out_specs=q_spec,
        out_shape=jax.ShapeDtypeStruct((B, H, T, HEAD_DIM), F32),
        compiler_params=_cparams(("parallel", "arbitrary")), name="moba_prompt",
    )(q_f32, k_bf, vt_bf, kmean)


def _mem_attn_kernel(q_ref, k_ref, v_ref, o_ref, *, n_heads):
    n_mem = k_ref.shape[1] // n_heads
    for h in range(n_heads):
        rows = pl.ds(h, n_mem, stride=n_heads)
        s = _dot_nt((q_ref[0, h] * ATTN_SCALE).astype(BF16), k_ref[0, rows, :].astype(BF16))
        m = jnp.max(s, axis=1, keepdims=True)
        p = jnp.exp(s - m)
        l = jnp.sum(p, axis=1, keepdims=True)
        o = jnp.dot(p.astype(BF16), v_ref[0, rows, :].astype(BF16), preferred_element_type=F32)
        o_ref[0, h] = o / l


def _mem_attn(q, mk, mv, tq):
    B, H, T, _ = q.shape
    n_mem = mk.shape[1] // H
    q_spec = pl.BlockSpec((1, H, tq, HEAD_DIM), lambda b, i: (b, 0, i, 0))
    kv_spec = pl.BlockSpec((1, n_mem * H, HEAD_DIM), lambda b, i: (b, 0, 0))
    return pl.pallas_call(
        functools.partial(_mem_attn_kernel, n_heads=H),
        grid=(B, T // tq),
        in_specs=[q_spec, kv_spec, kv_spec],
        out_specs=q_spec,
        out_shape=jax.ShapeDtypeStruct((B, H, T, HEAD_DIM), F32),
        compiler_params=_cparams(("parallel", "arbitrary")), name="mem_attn",
    )(q, mk, mv)


def _merge_kernel(x_ref, oa_ref, ob_ref, om_ref, ga_ref, gb_ref, gm_ref,
                  na_ref, nb_ref, nm_ref, w_ref, lg_ref, lb_ref, y_ref, *, alpha):
    def group(o_ref, g_ref, n_ref):
        o = jnp.concatenate([_heads_to_lanes(o_ref, (bb,)) for bb in range(o_ref.shape[0])],
                            axis=0)
        r = o * lax.rsqrt(jnp.mean(o * o, axis=-1, keepdims=True) + RMS_EPS) * n_ref[...]
        g = g_ref[...]
        return (r * (g * (1.0 / (1.0 + jnp.exp(-g))))).astype(BF16)

    mix = jnp.concatenate([group(oa_ref, ga_ref, na_ref), group(ob_ref, gb_ref, nb_ref),
                           group(om_ref, gm_ref, nm_ref)], axis=1)
    sub = jnp.dot(mix, w_ref[...], preferred_element_type=F32)
    h = alpha * x_ref[...] + sub
    mu = jnp.mean(h, axis=-1, keepdims=True)
    d = h - mu
    var = jnp.mean(d * d, axis=-1, keepdims=True)
    y_ref[...] = d * lax.rsqrt(var + LN_EPS) * lg_ref[...] + lb_ref[...]


def _merge(x2d, o_a, o_b, o_m, g_a, g_b, g_m, norm_a, norm_b, norm_m, w_out_bf, ln_g, ln_b,
           alpha, tm):
    M, D = x2d.shape
    seq = o_a.shape[2]
    tpb = max(seq // tm, 1)
    rpb = min(tm, seq)
    row = lambda a: pl.BlockSpec((tm, a.shape[1]), lambda i: (i, 0))
    head = lambda a: pl.BlockSpec((tm // rpb, a.shape[1], rpb, HEAD_DIM),
                                  lambda i: (i // tpb, 0, i % tpb, 0))
    full = lambda a: pl.BlockSpec(a.shape, lambda i: (0, 0))
    vecs = [v.reshape(1, -1) for v in (norm_a, norm_b, norm_m, ln_g, ln_b)]
    args = [x2d, o_a, o_b, o_m, g_a, g_b, g_m, *vecs[:3], w_out_bf, *vecs[3:]]
    in_specs = ([row(x2d)] + [head(a) for a in (o_a, o_b, o_m)] + [row(a) for a in (g_a, g_b, g_m)]
                + [full(a) for a in args[7:]])
    return pl.pallas_call(
        functools.partial(_merge_kernel, alpha=alpha),
        grid=(M // tm,), in_specs=in_specs,
        out_specs=pl.BlockSpec((tm, D), lambda i: (i, 0)),
        out_shape=jax.ShapeDtypeStruct((M, D), F32),
        compiler_params=_cparams(("parallel",)), name="merge",
    )(*args)


def _block_diag_queries(q_ref):
    _, H, Tq, d = q_ref.shape
    zero = jnp.zeros((Tq, d), q_ref.dtype)
    return jnp.concatenate(
        [jnp.concatenate([q_ref[0, h] if hh == h else zero for hh in range(H)], axis=1)
         for h in range(H)], axis=0)


def _new_token_rows(t_ref):
    rows = _heads_to_lanes(t_ref, (0,))
    pad = jnp.zeros((LANES - rows.shape[0], rows.shape[1]), rows.dtype)
    return jnp.concatenate([rows, pad], axis=0).astype(BF16)


def _page_rows(page_refs, dtype):
    return jnp.concatenate([_heads_to_lanes(r).astype(dtype) for r in page_refs], axis=0)


def _take_block_diag(acc, o_ref, n_heads, tq):
    for h in range(n_heads):
        o_ref[0, h] = acc[h * tq:(h + 1) * tq, h * HEAD_DIM:(h + 1) * HEAD_DIM]


def _sb_sample_kernel(pt_ref, q_ref, *rest, n_heads, tq, n_steps, n_pg):
    k_refs, v_refs = rest[:n_pg], rest[n_pg:2 * n_pg]
    kn_ref, vn_ref, u_ref, un_ref, o_ref, carry_ref, acc_ref = rest[2 * n_pg:]
    p = pl.program_id(1)
    q = _block_diag_queries(q_ref).astype(BF16)
    rows = n_heads * tq

    @pl.when(p == 0)
    def _():
        carry_ref[...] = jnp.zeros_like(carry_ref)
        acc_ref[...] = jnp.zeros_like(acc_ref)
        row = lax.broadcasted_iota(jnp.int32, (rows, LANES), 0)
        col = lax.broadcasted_iota(jnp.int32, (rows, LANES), 1)
        _sb_tile(_sb_logits([q], [_new_token_rows(kn_ref)]), [_new_token_rows(vn_ref)],
                 un_ref[...], carry_ref, acc_ref, col < row % tq)

    _sb_tile(_sb_logits([q], [_page_rows(k_refs, BF16)]), [_page_rows(v_refs, BF16)],
             u_ref[...], carry_ref, acc_ref, None)

    @pl.when(p == n_steps - 1)
    def _():
        _take_block_diag(acc_ref[...], o_ref, n_heads, tq)


def _sb_sample(q, k_pool, v_pool, k_new, v_new, page_table):
    B, H, tq, _ = q.shape
    rows, W = H * tq, H * HEAD_DIM
    page = k_pool.shape[2]
    n_pg = SAMPLE_PAGES_PER_STEP
    n_steps = page_table.shape[1] // n_pg
    assert n_steps * n_pg == page_table.shape[1]

    def pg(j):
        return pl.BlockSpec((None, H, page, HEAD_DIM),
                            lambda b, p, pt: (pt[b, n_pg * (n_steps - 1 - p) + j], 0, 0, 0))

    per_b = pl.BlockSpec((1, H, tq, HEAD_DIM), lambda b, p, pt: (b, 0, 0, 0))
    u2, un2 = _suffix_matrix(2 * page), _suffix_matrix(LANES)
    const = lambda a: pl.BlockSpec(a.shape, lambda b, p, pt: (0, 0))
    pages = [pg(j) for j in range(n_pg)]
    grid_spec = pltpu.PrefetchScalarGridSpec(
        num_scalar_prefetch=1, grid=(B, n_steps),
        in_specs=[per_b, *pages, *pages, per_b, per_b, const(u2), const(un2)],
        out_specs=per_b,
        scratch_shapes=[pltpu.VMEM((rows, LANES), F32), pltpu.VMEM((rows, W), F32)])
    return pl.pallas_call(
        functools.partial(_sb_sample_kernel, n_heads=H, tq=tq, n_steps=n_steps, n_pg=n_pg),
        grid_spec=grid_spec, out_shape=jax.ShapeDtypeStruct((B, H, tq, HEAD_DIM), F32),
        compiler_params=_cparams(("parallel", "arbitrary")), name="sb_sample",
    )(page_table, q, *([k_pool] * n_pg), *([v_pool] * n_pg), k_new, v_new, u2, un2)


def _moba_sample_kernel(pt_ref, qf_ref, *rest, n_heads, tq, n_steps, n_pg):
    k_refs, v_refs = rest[:n_pg], rest[n_pg:2 * n_pg]
    kn_ref, vn_ref, o_ref, km_ref, m_ref, l_ref, ob_ref = rest[2 * n_pg:]
    n = pl.program_id(1)
    rows = n_heads * tq
    blk = MOBA_BLOCK
    bps = n_pg // 2
    nb = n_steps * bps
    qf = _block_diag_queries(qf_ref)
    q = (qf * ATTN_SCALE).astype(BF16)
    lane = lax.broadcasted_iota(jnp.int32, (rows, LANES), 1)
    diag = lambda o: jnp.concatenate(
        [o[h * tq:(h + 1) * tq, h * HEAD_DIM:(h + 1) * HEAD_DIM] for h in range(n_heads)], axis=0)

    @pl.when(n == 0)
    def _():
        km_ref[...] = jnp.zeros_like(km_ref)
        m_ref[...] = jnp.zeros_like(m_ref)
        l_ref[...] = jnp.zeros_like(l_ref)

    pages = [_heads_to_lanes(r) for r in k_refs]
    kc = jnp.concatenate([pg.astype(BF16) for pg in pages], axis=0)
    s = _dot_nt(q, kc)
    m_all, l_all = m_ref[...], l_ref[...]
    for c in range(bps):
        j = n * bps + c
        km_ref[pl.ds(j, 1), :] = (jnp.sum(pages[2 * c], axis=0, keepdims=True)
                                  + jnp.sum(pages[2 * c + 1], axis=0, keepdims=True)) * (1.0 / blk)
        sc = s[:, c * blk:(c + 1) * blk]
        mc = jnp.max(sc, axis=1, keepdims=True)
        pc = jnp.exp(sc - mc)
        ob_ref[j] = diag(jnp.dot(pc.astype(BF16), _page_rows(v_refs[2 * c:2 * c + 2], BF16),
                                 preferred_element_type=F32))
        m_all = jnp.where(lane == j, mc, m_all)
        l_all = jnp.where(lane == j, jnp.sum(pc, axis=1, keepdims=True), l_all)
    m_ref[...] = m_all
    l_ref[...] = l_all

    @pl.when(n == n_steps - 1)
    def _():
        gate = _dot_nt(qf, km_ref[...], precision=lax.Precision.HIGHEST)
        sel = _select_topk(gate, lane < nb) > 0.0
        row = lax.broadcasted_iota(jnp.int32, (rows, LANES), 0)
        s_new = jnp.where(lane <= row % tq, _dot_nt(q, _new_token_rows(kn_ref)), NEG_BIG)
        m_blk = m_ref[...]
        m = jnp.maximum(jnp.max(s_new, axis=1, keepdims=True),
                        jnp.max(jnp.where(sel, m_blk, NEG_BIG), axis=1, keepdims=True))
        p_new = jnp.exp(s_new - m)
        w = jnp.where(sel, jnp.exp(jnp.where(sel, m_blk, NEG_BIG) - m), 0.0)
        l = (jnp.sum(p_new, axis=1, keepdims=True)
             + jnp.sum(w * l_ref[...], axis=1, keepdims=True))
        acc = diag(jnp.dot(p_new.astype(BF16), _new_token_rows(vn_ref),
                           preferred_element_type=F32))
        for j in range(nb):
            acc = acc + w[:, j:j + 1] * ob_ref[j]
        o = acc / l
        for h in range(n_heads):
            o_ref[0, h] = o[h * tq:(h + 1) * tq]


def _moba_sample(q, k_pool, v_pool, k_new, v_new, page_table):
    B, H, tq, _ = q.shape
    rows, W = H * tq, H * HEAD_DIM
    page = k_pool.shape[2]
    assert 2 * page == MOBA_BLOCK
    n_pg = SAMPLE_PAGES_PER_STEP
    n_steps = page_table.shape[1] // n_pg
    assert n_steps * n_pg == page_table.shape[1] and n_pg % 2 == 0
    nb = page_table.shape[1] // 2
    assert nb <= LANES

    def pg(j):
        return pl.BlockSpec((None, H, page, HEAD_DIM),
                            lambda b, n, pt: (pt[b, n_pg * n + j], 0, 0, 0))

    per_b = pl.BlockSpec((1, H, tq, HEAD_DIM), lambda b, n, pt: (b, 0, 0, 0))
    pages = [pg(j) for j in range(n_pg)]
    grid_spec = pltpu.PrefetchScalarGridSpec(
        num_scalar_prefetch=1, grid=(B, n_steps),
        in_specs=[per_b, *pages, *pages, per_b, per_b],
        out_specs=per_b,
        scratch_shapes=[pltpu.VMEM((LANES, W), F32),
                        pltpu.VMEM((rows, LANES), F32),
                        pltpu.VMEM((rows, LANES), F32),
                        pltpu.VMEM((nb, rows, HEAD_DIM), F32)])
    return pl.pallas_call(
        functools.partial(_moba_sample_kernel, n_heads=H, tq=tq, n_steps=n_steps, n_pg=n_pg),
        grid_spec=grid_spec, out_shape=jax.ShapeDtypeStruct((B, H, tq, HEAD_DIM), F32),
        compiler_params=_cparams(("parallel", "arbitrary")), name="moba_sample",
    )(page_table, q, *([k_pool] * n_pg), *([v_pool] * n_pg), k_new, v_new)


def _rope_tables(pos):
    inv = ROPE_THETA ** (-jnp.arange(0, HEAD_DIM, 2, dtype=F32) / HEAD_DIM)
    ang = pos.astype(F32)[:, None] * inv[None, :]
    cos, sin = jnp.cos(ang), jnp.sin(ang)
    return jnp.concatenate([cos, cos], -1), jnp.concatenate([-sin, sin], -1)


def _project_all(x2d, w_in_bf, w_sb, w_moba, w_mem, tm, seq, rope_tabs, prompt):
    assert w_sb == w_moba and (8 * w_sb) % (2 * w_mem) == 0
    grp = 4 * w_sb
    kv = (F32, BF16) if prompt else (F32,)
    seg = lambda k, width, rope=False, rows=(), heads=(), headt=(), km=False, scale=None: (
        k * width, width, rope, rows, heads, headt, km, scale)
    a = _project(x2d, w_in_bf, 0, grp,
                 (seg(0, w_sb, heads=(BF16,) if prompt else (F32,), scale=ATTN_SCALE),
                  seg(1, w_sb, heads=kv),
                  seg(2, w_sb, heads=kv), seg(3, w_sb, rows=(F32,))), tm, seq, seg_steps=not prompt)
    b = _project(x2d, w_in_bf, 1, grp,
                 (seg(0, w_moba, rope=True, heads=(F32,)),
                  seg(1, w_moba, rope=True, heads=kv, km=prompt),
                  seg(2, w_moba, heads=(F32,), headt=(BF16,) if prompt else ()),
                  seg(3, w_moba, rows=(F32,))), tm, seq, rope_tabs=rope_tabs,
                 seg_steps=not prompt)
    qm, gm = _project(x2d, w_in_bf, (2 * grp) // (2 * w_mem), 2 * w_mem,
                      (seg(0, w_mem, heads=(F32,)), seg(1, w_mem, rows=(F32,))), tm, seq,
                      seg_steps=not prompt)
    if prompt:
        names_a = ("qa", "ka", "ka_bf", "va", "va_bf", "ga")
        names_b = ("qb", "kb", "kb_bf", "kmean", "vb", "vbt_bf", "gb")
    else:
        names_a = ("qa", "ka", "va", "ga")
        names_b = ("qb", "kb", "vb", "gb")
    out = dict(zip(names_a, a))
    out.update(zip(names_b, b))
    out.update(qm=qm, gm=gm)
    return out


def kernel(x_prompt, x_sample, cache_sb_k, cache_sb_v, cache_moba_k, cache_moba_v,
           cache_mem_k, cache_mem_v, page_table, mem_prompt,
           w_in, w_mem_k, w_mem_v, norm_a, norm_b, norm_m, w_out, ln_g, ln_b):
    depth = w_in.shape[0]
    B, T, D = x_prompt.shape
    Bs, Ts, _ = x_sample.shape
    h_sb, h_moba, h_mem = cache_sb_k.shape[3], cache_moba_k.shape[3], cache_mem_k.shape[3]
    w_sb, w_moba, w_mem = h_sb * HEAD_DIM, h_moba * HEAD_DIM, h_mem * HEAD_DIM
    n_mem = mem_prompt.shape[1]
    page = cache_sb_k.shape[2]
    past_len = page_table.shape[1] * page
    assert past_len % MOBA_BLOCK == 0 and Ts <= LANES and T % MOBA_BLOCK == 0
    alpha = (2.0 * depth) ** 0.25

    tm = 512
    rope_p = _rope_tables(jnp.arange(T, dtype=jnp.int32))
    rope_s = tuple(jnp.tile(t, (Bs, 1)) for t in
                   _rope_tables(past_len + jnp.arange(Ts, dtype=jnp.int32)))

    y_p = x_prompt.reshape(B * T, D)
    y_s = x_sample.reshape(Bs * Ts, D)
    outs = [[] for _ in range(10)]
    for l in range(depth):
        w_in_bf = w_in[l].astype(BF16)
        w_out_bf = w_out[l].astype(BF16)
        w_memkv_bf = jnp.concatenate([w_mem_k[l], w_mem_v[l]], axis=1).astype(BF16)

        pp = _project_all(y_p, w_in_bf, w_sb, w_moba, w_mem, tm, T, rope_p, True)
        o_a = _sb_prompt(pp["qa"], pp["ka_bf"], pp["va_bf"])
        kmean = jnp.transpose(pp["kmean"], (0, 2, 1, 3, 4)).reshape(
            B, h_moba, T // MOBA_BLOCK, HEAD_DIM)
        o_b = _moba_prompt(pp["qb"], pp["kb_bf"], pp["vbt_bf"], kmean)
        mk, mv = _project(
            mem_prompt.reshape(B * n_mem, D), w_memkv_bf, 0, 2 * w_mem,
            ((0, w_mem, False, (F32,), (), (), False, None),
             (w_mem, w_mem, False, (F32,), (), (), False, None)),
            256, n_mem, interleave=True)
        tok_head = lambda a: a.reshape(a.shape[0], n_mem * h_mem, HEAD_DIM)
        o_m = _mem_attn(pp["qm"], mk.reshape(B, n_mem * h_mem, HEAD_DIM),
                        mv.reshape(B, n_mem * h_mem, HEAD_DIM), 1024)
        y_p_new = _merge(y_p, o_a, o_b, o_m, pp["ga"], pp["gb"], pp["gm"],
                         norm_a[l], norm_b[l], norm_m[l], w_out_bf, ln_g[l], ln_b[l], alpha, 512)

        ps = _project_all(y_s, w_in_bf, w_sb, w_moba, w_mem, Bs * Ts, Ts, rope_s, False)
        pool = lambda c: jnp.transpose(c[l], (0, 2, 1, 3))
        so_a = _sb_sample(ps["qa"], pool(cache_sb_k), pool(cache_sb_v), ps["ka"], ps["va"],
                          page_table)
        so_b = _moba_sample(ps["qb"], pool(cache_moba_k), pool(cache_moba_v), ps["kb"],
                            ps["vb"], page_table)
        so_m = _mem_attn(ps["qm"], tok_head(cache_mem_k[l]), tok_head(cache_mem_v[l]), Ts)
        y_s_new = _merge(y_s, so_a, so_b, so_m, ps["ga"], ps["gb"], ps["gm"],
                         norm_a[l], norm_b[l], norm_m[l], w_out_bf, ln_g[l], ln_b[l], alpha,
                         Bs * Ts)

        tok_major = lambda a: jnp.transpose(a, (0, 2, 1, 3))
        new = [tok_major(pp["ka"]), tok_major(pp["va"]), tok_major(pp["kb"]), tok_major(pp["vb"]),
               mk.reshape(B, n_mem, h_mem, HEAD_DIM), mv.reshape(B, n_mem, h_mem, HEAD_DIM),
               tok_major(ps["ka"]), tok_major(ps["va"]), tok_major(ps["kb"]), tok_major(ps["vb"])]
        for lst, a in zip(outs, new):
            lst.append(a)
        y_p, y_s = y_p_new, y_s_new

    return (y_p.reshape(B, T, D), y_s.reshape(Bs, Ts, D), *[jnp.stack(o) for o in outs])
```
